```python
import math
import jax, jax.numpy as jnp
from jax import lax
import numpy as np

D_MODEL = 1024
BATCH = 2
SEQ = 8192
DEPTH = 2
DEC_BATCH = 32
DEC_SEQ = 1
PAST_LEN = 8192
PAGE_SIZE = 128

HEAD_DIM = 64
N_FOX_HEADS = 8
N_DIFF_HEADS = 4
DIFF_V_DIM = 2 * HEAD_DIM
FOX_WIDTH = N_FOX_HEADS * HEAD_DIM
DIFF_QK_WIDTH = N_DIFF_HEADS * 2 * HEAD_DIM
DIFF_WIDTH = N_DIFF_HEADS * DIFF_V_DIM
MIX_WIDTH = FOX_WIDTH + DIFF_WIDTH
IN_WIDTH = 3 * FOX_WIDTH + N_FOX_HEADS + 2 * DIFF_QK_WIDTH + DIFF_WIDTH
N_EXPERTS = 16
N_GROUPS = 4
EXPERTS_PER_GROUP = N_EXPERTS // N_GROUPS
TOP_K = 2
D_EXPERT = D_MODEL // 2
Q_BLOCK = 128
DN_ALPHA = (2 * DEPTH) ** 0.25
DN_BETA = (8 * DEPTH) ** -0.25
LN_EPS = 1e-5
RMS_EPS = 1e-5

kernel_name = "fox_diffattn_hybrid_shared_router_moe_step"

F32 = jnp.float32


def layer_norm(x, g, b):
    xf = x.astype(F32)
    mu = jnp.mean(xf, -1, keepdims=True)
    var = jnp.mean(jnp.square(xf - mu), -1, keepdims=True)
    return ((xf - mu) * lax.rsqrt(var + LN_EPS) * g.astype(F32) + b.astype(F32)).astype(x.dtype)


def alibi_slopes(n):
    return jnp.asarray([2.0 ** (-8.0 * (i + 1) / n) for i in range(n)], F32)


def diff_lambda_init(layer):
    return 0.8 - 0.6 * math.exp(-0.3 * layer)


def project(h, w_in, b_forget):
    B, T, _ = h.shape
    p = h @ w_in
    cuts = [FOX_WIDTH, 2 * FOX_WIDTH, 3 * FOX_WIDTH, 3 * FOX_WIDTH + N_FOX_HEADS,
            3 * FOX_WIDTH + N_FOX_HEADS + DIFF_QK_WIDTH,
            3 * FOX_WIDTH + N_FOX_HEADS + 2 * DIFF_QK_WIDTH]
    fq, fk, fv, ff, dq, dk, dv = jnp.split(p, cuts, axis=-1)
    fq = fq.reshape(B, T, N_FOX_HEADS, HEAD_DIM)
    fk = fk.reshape(B, T, N_FOX_HEADS, HEAD_DIM)
    fv = fv.reshape(B, T, N_FOX_HEADS, HEAD_DIM)
    logf = jax.nn.log_sigmoid(ff.astype(F32) + b_forget.astype(F32))
    dq = dq.reshape(B, T, N_DIFF_HEADS, 2, HEAD_DIM)
    dk = dk.reshape(B, T, N_DIFF_HEADS, 2 * HEAD_DIM)
    dv = dv.reshape(B, T, N_DIFF_HEADS, DIFF_V_DIM)
    return fq, fk, fv, logf, dq, dk, dv


def fox_block(qpos, q, cq, k, v, kpos, ck):
    s = jnp.einsum('bqhd,bkhd->bhqk', q, k, preferred_element_type=F32) * (HEAD_DIM ** -0.5)
    s = s + (jnp.transpose(cq, (0, 2, 1))[..., :, None] - jnp.transpose(ck, (0, 2, 1))[..., None, :])
    mask = kpos[None, :] <= qpos[:, None]
    p = jax.nn.softmax(jnp.where(mask, s, -jnp.inf), axis=-1)
    return jnp.einsum('bhqk,bkhd->bqhd', p.astype(v.dtype), v, preferred_element_type=F32)


def diff_block(qpos, q, k, v, kpos, slopes, lam):
    B, K = k.shape[:2]
    k2 = k.reshape(B, K, N_DIFF_HEADS, 2, HEAD_DIM)
    s = jnp.einsum('bqhcd,bkhcd->bchqk', q, k2, preferred_element_type=F32) * (HEAD_DIM ** -0.5)
    dist = (qpos[:, None] - kpos[None, :]).astype(F32)
    s = s - slopes[:, None, None] * dist[None]
    mask = kpos[None, :] <= qpos[:, None]
    p = jax.nn.softmax(jnp.where(mask, s, -jnp.inf), axis=-1)
    a = p[:, 0] - lam * p[:, 1]
    return jnp.einsum('bhqk,bkhe->bqhe', a.astype(v.dtype), v, preferred_element_type=F32)


def sweep_query_blocks(fn, qpos, q_arrays):
    Q = qpos.shape[0]
    if Q <= Q_BLOCK or Q % Q_BLOCK:
        return fn(qpos, *q_arrays)
    nb = Q // Q_BLOCK
    blk = lambda a: jnp.moveaxis(a.reshape(a.shape[0], nb, Q_BLOCK, *a.shape[2:]), 1, 0)
    out = lax.map(lambda args: fn(args[0], *args[1]),
                  (qpos.reshape(nb, Q_BLOCK), tuple(blk(a) for a in q_arrays)))
    out = jnp.moveaxis(out, 0, 1)
    return out.reshape(out.shape[0], Q, *out.shape[3:])


def attention_sublayer(h, past, w_in, b_forget, w_out, lq1, lk1, lq2, lk2, subln_g, lambda_init):
    B, T, _ = h.shape
    fq, fk, fv, logf, dq, dk, dv = project(h, w_in, b_forget)
    new_rows = (fk, fv, logf, dk, dv)
    if past is None:
        k_f, v_f, lf, k_d, v_d = new_rows
    else:
        k_f, v_f, lf, k_d, v_d = (jnp.concatenate([p_, n_], axis=1) for p_, n_ in zip(past, new_rows))
    L = k_f.shape[1]
    kpos = jnp.arange(L, dtype=jnp.int32)
    qpos = (L - T) + jnp.arange(T, dtype=jnp.int32)
    c = jnp.cumsum(lf.astype(F32), axis=1)
    cq = c[:, L - T:]
    fox = sweep_query_blocks(lambda qp, q, cqb: fox_block(qp, q, cqb, k_f, v_f, kpos, c), qpos, (fq, cq))
    lam = (jnp.exp(jnp.sum(lq1.astype(F32) * lk1.astype(F32)))
           - jnp.exp(jnp.sum(lq2.astype(F32) * lk2.astype(F32))) + lambda_init)
    slopes = alibi_slopes(N_DIFF_HEADS)
    diff = sweep_query_blocks(lambda qp, q: diff_block(qp, q, k_d, v_d, kpos, slopes, lam), qpos, (dq,))
    diff = (diff * lax.rsqrt(jnp.mean(jnp.square(diff), -1, keepdims=True) + RMS_EPS)
            * subln_g.astype(F32) * (1.0 - lambda_init))
    mixed = jnp.concatenate([fox.reshape(B, T, FOX_WIDTH), diff.reshape(B, T, DIFF_WIDTH)], -1).astype(h.dtype)
    return mixed @ w_out, new_rows


def moe(h, w_router, b_router, w_gate, w_up, w_down):
    B, T, D = h.shape
    x = h.reshape(B * T, D)
    logits = (x @ w_router).astype(F32) + b_router.astype(F32)
    scores = jax.nn.softmax(logits, axis=-1).reshape(-1, N_GROUPS, EXPERTS_PER_GROUP)
    top_v, top_i = lax.top_k(scores, TOP_K)
    g = jnp.argmax(jnp.sum(top_v, -1), axis=-1)
    sel_v = jnp.take_along_axis(top_v, g[:, None, None], axis=1)[:, 0]
    sel_i = jnp.take_along_axis(top_i, g[:, None, None], axis=1)[:, 0] + g[:, None] * EXPERTS_PER_GROUP
    gates = sel_v / jnp.sum(sel_v, -1, keepdims=True)
    combine = jnp.einsum('nk,nke->ne', gates, jax.nn.one_hot(sel_i, N_EXPERTS, dtype=F32))
    hg = jnp.einsum('nd,edf->nef', x, w_gate)
    hu = jnp.einsum('nd,edf->nef', x, w_up)
    act = jax.nn.silu(hg) * hu * combine[:, :, None].astype(hg.dtype)
    y = jnp.einsum('nef,efd->nd', act, w_down)
    return y.reshape(B, T, D).astype(h.dtype)


def gather_pages(pool, page_table):
    g = pool[page_table]
    return g.reshape(g.shape[0], g.shape[1] * g.shape[2], *g.shape[3:])


def trunk_layer(h, past, layer, w_in, b_forget, w_out, lambda_q1, lambda_k1, lambda_q2, lambda_k2,
                subln_g, ln1_g, ln1_b, ln2_g, ln2_b, w_router, b_router, w_gate, w_up, w_down):
    a, new_rows = attention_sublayer(h, past, w_in[layer], b_forget[layer], w_out[layer],
                                     lambda_q1[layer], lambda_k1[layer], lambda_q2[layer], lambda_k2[layer],
                                     subln_g[layer], diff_lambda_init(layer))
    h = layer_norm(DN_ALPHA * h + a, ln1_g[layer], ln1_b[layer])
    m = moe(h, w_router, b_router, w_gate[layer], w_up[layer], w_down[layer])
    h = layer_norm(DN_ALPHA * h + m, ln2_g[layer], ln2_b[layer])
    return h, new_rows


def setup_inputs(seed: int = 0) -> dict:
    key = jax.random.key(seed)
    ks = jax.random.split(key, 24)
    n_pages = PAST_LEN // PAGE_SIZE
    n_pool = (5 * DEC_BATCH * n_pages) // 4
    nrm = lambda k, shape, s: s * jax.random.normal(k, shape, F32)
    col = jnp.arange(IN_WIDTH)
    v_fox = (col >= 2 * FOX_WIDTH) & (col < 3 * FOX_WIDTH)
    v_diff = col >= IN_WIDTH - DIFF_WIDTH
    col_scale = jnp.where(v_fox | v_diff, DN_BETA, 1.0).astype(F32)
    return {
        "x_prompt": nrm(ks[0], (BATCH, SEQ, D_MODEL), 1.0),
        "x_sample": nrm(ks[1], (DEC_BATCH, DEC_SEQ, D_MODEL), 1.0),
        "cache_fox_k": nrm(ks[2], (DEPTH, n_pool, PAGE_SIZE, N_FOX_HEADS, HEAD_DIM), 1.0),
        "cache_fox_v": nrm(ks[3], (DEPTH, n_pool, PAGE_SIZE, N_FOX_HEADS, HEAD_DIM), 0.5),
        "cache_fox_logf": jax.nn.log_sigmoid(4.0 + nrm(ks[4], (DEPTH, n_pool, PAGE_SIZE, N_FOX_HEADS), 1.0)),
        "cache_diff_k": nrm(ks[5], (DEPTH, n_pool, PAGE_SIZE, N_DIFF_HEADS, 2 * HEAD_DIM), 1.0),
        "cache_diff_v": nrm(ks[6], (DEPTH, n_pool, PAGE_SIZE, N_DIFF_HEADS, DIFF_V_DIM), 0.5),
        "page_table": jax.random.permutation(ks[7], n_pool)[:DEC_BATCH * n_pages]
                      .reshape(DEC_BATCH, n_pages).astype(jnp.int32),
        "w_in": nrm(ks[8], (DEPTH, D_MODEL, IN_WIDTH), D_MODEL ** -0.5) * col_scale,
        "b_forget": 4.0 + nrm(ks[9], (DEPTH, N_FOX_HEADS), 0.5),
        "w_out": nrm(ks[10], (DEPTH, MIX_WIDTH, D_MODEL), MIX_WIDTH ** -0.5 * DN_BETA),
        "lambda_q1": nrm(ks[11], (DEPTH, HEAD_DIM), 0.1),
        "lambda_k1": nrm(ks[12], (DEPTH, HEAD_DIM), 0.1),
        "lambda_q2": nrm(ks[13], (DEPTH, HEAD_DIM), 0.1),
        "lambda_k2": nrm(ks[14], (DEPTH, HEAD_DIM), 0.1),
        "subln_g": 1.0 + nrm(ks[15], (DEPTH, DIFF_V_DIM), 0.02),
        "ln1_g": 1.0 + nrm(ks[16], (DEPTH, D_MODEL), 0.02),
        "ln1_b": nrm(ks[17], (DEPTH, D_MODEL), 0.02),
        "ln2_g": 1.0 + nrm(ks[18], (DEPTH, D_MODEL), 0.02),
        "ln2_b": nrm(ks[19], (DEPTH, D_MODEL), 0.02),
        "w_router": nrm(ks[20], (D_MODEL, N_EXPERTS), D_MODEL ** -0.5),
        "b_router": nrm(ks[21], (N_EXPERTS,), 0.01),
        "w_gate": nrm(ks[22], (DEPTH, N_EXPERTS, D_MODEL, D_EXPERT), D_MODEL ** -0.5),
        "w_up": nrm(jax.random.fold_in(ks[22], 1), (DEPTH, N_EXPERTS, D_MODEL, D_EXPERT), D_MODEL ** -0.5),
        "w_down": nrm(ks[23], (DEPTH, N_EXPERTS, D_EXPERT, D_MODEL), D_EXPERT ** -0.5 * DN_BETA),
    }


def reference(x_prompt, x_sample, cache_fox_k, cache_fox_v, cache_fox_logf, cache_diff_k, cache_diff_v,
              page_table, w_in, b_forget, w_out, lambda_q1, lambda_k1, lambda_q2, lambda_k2, subln_g,
              ln1_g, ln1_b, ln2_g, ln2_b, w_router, b_router, w_gate, w_up, w_down):
    caches = (cache_fox_k, cache_fox_v, cache_fox_logf, cache_diff_k, cache_diff_v)
    hp, hs = x_prompt, x_sample
    p_rows = [[] for _ in range(5)]
    s_rows = [[] for _ in range(5)]
    for layer in range(DEPTH):
        hp, new_p = trunk_layer(hp, None, layer, w_in, b_forget, w_out, lambda_q1, lambda_k1, lambda_q2,
                                lambda_k2, subln_g, ln1_g, ln1_b, ln2_g, ln2_b, w_router, b_router,
                                w_gate, w_up, w_down)
        past = tuple(gather_pages(c[layer], page_table) for c in caches)
        hs, new_s = trunk_layer(hs, past, layer, w_in, b_forget, w_out, lambda_q1, lambda_k1, lambda_q2,
                                lambda_k2, subln_g, ln1_g, ln1_b, ln2_g, ln2_b, w_router, b_router,
                                w_gate, w_up, w_down)
        for i in range(5):
            p_rows[i].append(new_p[i])
            s_rows[i].append(new_s[i])
    pk, pv, plf, pdk, pdv = (jnp.stack(r, 0) for r in p_rows)
    sk, sv, slf, sdk, sdv = (jnp.stack(r, 0) for r in s_rows)
    return (hp, hs, pk, pv, plf, pdk, pdv, sk, sv, slf, sdk, sdv)
```

```python
import functools
import math

import jax
import jax.numpy as jnp
from jax import lax
from jax.experimental import pallas as pl
from jax.experimental.pallas import tpu as pltpu

F32 = jnp.float32
BF16 = jnp.bfloat16

HEAD_DIM = 64
N_FOX_HEADS = 8
N_DIFF_HEADS = 4
FOX_WIDTH = N_FOX_HEADS * HEAD_DIM
DIFF_WIDTH = N_DIFF_HEADS * 2 * HEAD_DIM
N_EXPERTS = 16
N_GROUPS = 4
EXPERTS_PER_GROUP = N_EXPERTS // N_GROUPS
LN_EPS = 1e-5
RMS_EPS = 1e-5
LOG2E = 1.4426950408889634
QK_SCALE = HEAD_DIM ** -0.5 * LOG2E
NEG = -1e30
LANES = 128
VMEM_LIMIT = 56 * 1024 * 1024


def _round8(x):
    t = x * 65537.0
    return t - (t - x)


def _split3(x):
    hi = _round8(x)
    r = x - hi
    mid = _round8(r)
    return hi, mid, r - mid


def _layer_norm(x, g, b):
    mu = jnp.mean(x, axis=-1, keepdims=True)
    xc = x - mu
    var = jnp.mean(xc * xc, axis=-1, keepdims=True)
    return xc * lax.rsqrt(var + LN_EPS) * g + b


def _nt_dot(a, b):
    return lax.dot_general(a, b, (((1,), (1,)), ((), ())), preferred_element_type=F32)


def _proj_kernel(x_ref, wm_ref, wf_ref, bf_ref, tri_ref, dkb_ref,
                 fq_ref, fkaug_ref, fvaug_ref, cqb_ref, dq_ref, dkaug_ref, dvaug_ref,
                 fk_ref, fv_ref, lf_ref, dk_ref, dv_ref, carry_ref, *, blocks_per_seq):
    i = pl.program_id(0)

    @pl.when(i % blocks_per_seq == 0)
    def _():
        carry_ref[...] = jnp.zeros_like(carry_ref)

    xb = x_ref[...].astype(BF16)
    p = jnp.dot(xb, wm_ref[...], preferred_element_type=F32)
    tm = p.shape[0]
    lane = lax.broadcasted_iota(jnp.int32, (tm, LANES), 1)

    fq_ref[...] = (p[:, 0:512] * QK_SCALE).astype(BF16)
    fk = p[:, 512:1024]
    fv = p[:, 1024:1536]
    dq_ref[...] = (p[:, 1536:2048] * QK_SCALE).astype(BF16)
    dk = p[:, 2048:2560]
    dv = p[:, 2560:3072]
    fk_ref[...] = fk
    fv_ref[...] = fv
    dk_ref[...] = dk
    dv_ref[...] = dv

    z = jnp.dot(xb, wf_ref[...], preferred_element_type=F32) + bf_ref[...]
    lf = jnp.minimum(z, 0.0) - jnp.log1p(jnp.exp(-jnp.abs(z)))
    lf = jnp.where(lane < 24, lf, 0.0)
    lf_ref[...] = lf[:, 0:8]

    hi, mid, lo = _split3(lf)
    packed = jnp.where(lane < 8, hi, jnp.where(lane < 16, mid, lo)).astype(BF16)
    zc = jnp.dot(tri_ref[...], packed, preferred_element_type=F32)
    c = zc + pltpu.roll(zc, 120, 1) + pltpu.roll(zc, 112, 1) + carry_ref[...]
    carry_ref[...] = c[tm - 1:tm, :]

    c2 = jnp.where(lane < 8, c * LOG2E, 0.0)
    c2rep = c2 + pltpu.roll(c2, 8, 1) + pltpu.roll(c2, 16, 1)
    h3, m3, l3 = _split3(c2rep)
    parts = jnp.where(lane < 8, h3, jnp.where(lane < 16, m3, l3))
    kparts = pltpu.roll(-parts, 32, 1)
    cqb = jnp.where((lane >= 32) & (lane < 56), 1.0, parts)
    kb = jnp.where(lane < 24, 1.0, kparts).astype(BF16)
    cqb_ref[...] = cqb.astype(BF16)

    ones = jnp.ones((tm, LANES), BF16)
    for j in range(4):
        sl = slice(LANES * j, LANES * (j + 1))
        fkaug_ref[j, :, 0:LANES] = fk[:, sl].astype(BF16)
        fkaug_ref[j, :, LANES:2 * LANES] = kb
        vp = fv[:, sl]
        fvaug_ref[2 * j] = jnp.where(lane < HEAD_DIM, vp, 1.0).astype(BF16)
        fvaug_ref[2 * j + 1] = jnp.where(lane >= HEAD_DIM, vp, 1.0).astype(BF16)
        dkaug_ref[j, :, 0:LANES] = dk[:, sl].astype(BF16)
        dkaug_ref[j, :, LANES:2 * LANES] = dkb_ref[j]
        dvaug_ref[j, :, 0:LANES] = dv[:, sl].astype(BF16)
        dvaug_ref[j, :, LANES:2 * LANES] = ones


def _project_prompt(x, wm, wf, bfv, tri, dkb, *, seq_len, tm):
    n, d = x.shape
    nb = n // tm
    bps = seq_len // tm
    row = lambda i: (i, 0)
    row3 = lambda i: (0, i, 0)
    const = lambda i: (0, 0)
    out_shape = (
        jax.ShapeDtypeStruct((n, 512), BF16),
        jax.ShapeDtypeStruct((4, n, 256), BF16),
        jax.ShapeDtypeStruct((8, n, 128), BF16),
        jax.ShapeDtypeStruct((n, 128), BF16),
        jax.ShapeDtypeStruct((n, 512), BF16),
        jax.ShapeDtypeStruct((4, n, 256), BF16),
        jax.ShapeDtypeStruct((4, n, 256), BF16),
        jax.ShapeDtypeStruct((n, 512), F32),
        jax.ShapeDtypeStruct((n, 512), F32),
        jax.ShapeDtypeStruct((n, 8), F32),
        jax.ShapeDtypeStruct((n, 512), F32),
        jax.ShapeDtypeStruct((n, 512), F32),
    )
    out_specs = (
        pl.BlockSpec((tm, 512), row),
        pl.BlockSpec((4, tm, 256), row3),
        pl.BlockSpec((8, tm, 128), row3),
        pl.BlockSpec((tm, 128), row),
        pl.BlockSpec((tm, 512), row),
        pl.BlockSpec((4, tm, 256), row3),
        pl.BlockSpec((4, tm, 256), row3),
        pl.BlockSpec((tm, 512), row),
        pl.BlockSpec((tm, 512), row),
        pl.BlockSpec((tm, 8), row),
        pl.BlockSpec((tm, 512), row),
        pl.BlockSpec((tm, 512), row),
    )
    in_specs = [
        pl.BlockSpec((tm, d), row),
        pl.BlockSpec(wm.shape, const),
        pl.BlockSpec(wf.shape, const),
        pl.BlockSpec(bfv.shape, const),
        pl.BlockSpec(tri.shape, const),
        pl.BlockSpec((4, tm, 128), lambda i: (0, i % bps, 0)),
    ]
    return pl.pallas_call(
        functools.partial(_proj_kernel, blocks_per_seq=bps),
        grid=(nb,), in_specs=in_specs, out_specs=out_specs, out_shape=out_shape,
        scratch_shapes=[pltpu.VMEM((1, LANES), F32)],
        compiler_params=pltpu.CompilerParams(dimension_semantics=("arbitrary",),
                                             vmem_limit_bytes=VMEM_LIMIT),
        name="proj_prompt",
    )(x, wm, wf, bfv, tri, dkb)


def _online_softmax_step(s, m_ref, acc_ref, idx, v):
    m_prev = m_ref[idx]
    m_new = jnp.maximum(m_prev, jnp.max(s, axis=1, keepdims=True))
    p = jnp.exp2(s - m_new).astype(BF16)
    alpha = jnp.exp2(m_prev - m_new)
    acc_ref[idx] = alpha * acc_ref[idx] + jnp.dot(p, v, preferred_element_type=F32)
    m_ref[idx] = m_new


def _causal_sweep(qi, blk, step):
    def body(ki, carry):
        step(ki, False)
        return carry
    lax.fori_loop(0, qi, body, 0)
    step(qi, True)


def _fox_attn_kernel(q_ref, cqb_ref, k_ref, v_ref, o_ref, qaug_ref, acc_ref, m_ref, *, blk):
    pair = pl.program_id(1)
    qi = pl.program_id(2)
    lane = lax.broadcasted_iota(jnp.int32, (blk, LANES), 1)
    q = q_ref[...].astype(F32)
    cqb = cqb_ref[...].astype(F32)
    bias_lanes = (lane < 24) | ((lane >= 32) & (lane < 56))
    for hh in range(2):
        h = 2 * pair + hh
        in_head = (lane >= HEAD_DIM * hh) & (lane < HEAD_DIM * (hh + 1))
        sel = ((lane & 7) == h) & bias_lanes
        qaug_ref[hh, :, 0:LANES] = jnp.where(in_head, q, 0.0).astype(BF16)
        qaug_ref[hh, :, LANES:2 * LANES] = jnp.where(sel, cqb, 0.0).astype(BF16)
    m_ref[...] = jnp.full(m_ref.shape, NEG, F32)
    acc_ref[...] = jnp.zeros(acc_ref.shape, F32)

    def step(ki, masked):
        start = pl.multiple_of(ki * blk, blk)
        kb = k_ref[pl.ds(start, blk), :]
        for hh in range(2):
            s = _nt_dot(qaug_ref[hh], kb)
            if masked:
                r = lax.broadcasted_iota(jnp.int32, (blk, blk), 0)
                c = lax.broadcasted_iota(jnp.int32, (blk, blk), 1)
                s = jnp.where(c <= r, s, NEG)
            _online_softmax_step(s, m_ref, acc_ref, hh, v_ref[hh, pl.ds(start, blk), :])

    _causal_sweep(qi, blk, step)
    a0 = acc_ref[0]
    a1 = acc_ref[1]
    o0 = a0 / pltpu.roll(a0, HEAD_DIM, 1)
    o1 = a1 / pltpu.roll(a1, HEAD_DIM, 1)
    o_ref[...] = jnp.where(lane < HEAD_DIM, o0, o1).astype(o_ref.dtype)


def _fox_attention(fq, cqb, fkaug, fvaug, *, batch, seq_len, blk):
    n = fq.shape[0]
    nq = seq_len // blk
    qmap = lambda b, p, i: (b * nq + i, p)
    return pl.pallas_call(
        functools.partial(_fox_attn_kernel, blk=blk),
        grid=(batch, 4, nq),
        in_specs=[
            pl.BlockSpec((blk, LANES), qmap),
            pl.BlockSpec((blk, LANES), lambda b, p, i: (b * nq + i, 0)),
            pl.BlockSpec((None, seq_len, 256), lambda b, p, i: (p, b, 0)),
            pl.BlockSpec((2, seq_len, LANES), lambda b, p, i: (p, b, 0)),
        ],
        out_specs=pl.BlockSpec((blk, LANES), qmap),
        out_shape=jax.ShapeDtypeStruct((n, FOX_WIDTH), BF16),
        scratch_shapes=[pltpu.VMEM((2, blk, 256), BF16), pltpu.VMEM((2, blk, LANES), F32),
                        pltpu.VMEM((2, blk, 1), F32)],
        compiler_params=pltpu.CompilerParams(
            dimension_semantics=("arbitrary", "arbitrary", "arbitrary"), vmem_limit_bytes=VMEM_LIMIT),
        name="fox_attn",
    )(fq, cqb, fkaug, fvaug)


def _diff_lambda(lq1, lk1, lq2, lk2, lambda_init):
    return (jnp.exp(jnp.sum(lq1 * lk1, axis=1, keepdims=True))
            - jnp.exp(jnp.sum(lq2 * lk2, axis=1, keepdims=True)) + lambda_init)


def _diff_combine(o1, o2, lam, g, lambda_init):
    a = o1 - lam * o2
    a = a * lax.rsqrt(jnp.mean(a * a, axis=-1, keepdims=True) + RMS_EPS)
    return a * g * (1.0 - lambda_init)


def _diff_attn_kernel(q_ref, qb_ref, k_ref, v_ref, lq1_ref, lk1_ref, lq2_ref, lk2_ref, g_ref,
                      o_ref, qaug_ref, acc_ref, m_ref, *, blk, lambda_init):
    qi = pl.program_id(2)
    lane = lax.broadcasted_iota(jnp.int32, (blk, LANES), 1)
    q = q_ref[...].astype(F32)
    for c in range(2):
        half = (lane >= HEAD_DIM * c) & (lane < HEAD_DIM * (c + 1))
        qaug_ref[c, :, 0:LANES] = jnp.where(half, q, 0.0).astype(BF16)
        qaug_ref[c, :, LANES:2 * LANES] = qb_ref[...]
    m_ref[...] = jnp.full(m_ref.shape, NEG, F32)
    acc_ref[...] = jnp.zeros(acc_ref.shape, F32)

    def step(ki, masked):
        start = pl.multiple_of(ki * blk, blk)
        kb = k_ref[pl.ds(start, blk), :]
        vb = v_ref[pl.ds(start, blk), :]
        for c in range(2):
            s = _nt_dot(qaug_ref[c], kb)
            if masked:
                r = lax.broadcasted_iota(jnp.int32, (blk, blk), 0)
                cc = lax.broadcasted_iota(jnp.int32, (blk, blk), 1)
                s = jnp.where(cc <= r, s, NEG)
            _online_softmax_step(s, m_ref, acc_ref, c, vb)

    _causal_sweep(qi, blk, step)
    a1 = acc_ref[0]
    a2 = acc_ref[1]
    o1 = a1[:, 0:LANES] / a1[:, LANES:2 * LANES]
    o2 = a2[:, 0:LANES] / a2[:, LANES:2 * LANES]
    lam = _diff_lambda(lq1_ref[...], lk1_ref[...], lq2_ref[...], lk2_ref[...], lambda_init)
    o_ref[...] = _diff_combine(o1, o2, lam, g_ref[...], lambda_init).astype(o_ref.dtype)


def _diff_attention(dq, dqb, dkaug, dvaug, lq1, lk1, lq2, lk2, g, *, batch, seq_len, blk, lambda_init):
    n = dq.shape[0]
    nq = seq_len // blk
    qmap = lambda b, h, i: (b * nq + i, h)
    vec = lambda b, h, i: (0, 0)
    return pl.pallas_call(
        functools.partial(_diff_attn_kernel, blk=blk, lambda_init=lambda_init),
        grid=(batch, N_DIFF_HEADS, nq),
        in_specs=[
            pl.BlockSpec((blk, LANES), qmap),
            pl.BlockSpec((None, blk, LANES), lambda b, h, i: (h, i, 0)),
            pl.BlockSpec((None, seq_len, 256), lambda b, h, i: (h, b, 0)),
            pl.BlockSpec((None, seq_len, 256), lambda b, h, i: (h, b, 0)),
            pl.BlockSpec((1, HEAD_DIM), vec), pl.BlockSpec((1, HEAD_DIM), vec),
            pl.BlockSpec((1, HEAD_DIM), vec), pl.BlockSpec((1, HEAD_DIM), vec),
            pl.BlockSpec((1, LANES), vec),
        ],
        out_specs=pl.BlockSpec((blk, LANES), qmap),
        out_shape=jax.ShapeDtypeStruct((n, DIFF_WIDTH), BF16),
        scratch_shapes=[pltpu.VMEM((2, blk, 256), BF16), pltpu.VMEM((2, blk, 256), F32),
                        pltpu.VMEM((2, blk, 1), F32)],
        compiler_params=pltpu.CompilerParams(
            dimension_semantics=("arbitrary", "arbitrary", "arbitrary"), vmem_limit_bytes=VMEM_LIMIT),
        name="diff_attn",
    )(dq, dqb, dkaug, dvaug, lq1, lk1, lq2, lk2, g)


def _route(logits, lane):
    lanef = lane.astype(F32)
    lg = jnp.where(lane < N_EXPERTS, logits, NEG)
    mx = jnp.max(lg, axis=1, keepdims=True)
    e = jnp.exp(lg - mx)
    scores = e / jnp.sum(e, axis=1, keepdims=True)
    best = None
    for g in range(N_GROUPS):
        ing = (lane >= EXPERTS_PER_GROUP * g) & (lane < EXPERTS_PER_GROUP * (g + 1))
        sg = jnp.where(ing, scores, -1.0)
        m1 = jnp.max(sg, axis=1, keepdims=True)
        i1 = jnp.min(jnp.where(sg == m1, lanef, 999.0), axis=1, keepdims=True)
        sg2 = jnp.where(lanef == i1, -1.0, sg)
        m2 = jnp.max(sg2, axis=1, keepdims=True)
        i2 = jnp.min(jnp.where(sg2 == m2, lanef, 999.0), axis=1, keepdims=True)
        cand = (m1 + m2, m1, m2, i1, i2)
        if best is None:
            best = cand
        else:
            upd = cand[0] > best[0]
            best = tuple(jnp.where(upd, cn, bs) for cn, bs in zip(cand, best))
    _, v1, v2, i1, i2 = best
    den = v1 + v2
    return jnp.where(lanef == i1, v1 / den, 0.0) + jnp.where(lanef == i2, v2 / den, 0.0)


def _outproj_kernel(fox_ref, diff_ref, h_ref, wo_ref, g_ref, b_ref, wr_ref, br_ref,
                    h1_ref, h1b_ref, comb_ref, *, dn_alpha):
    mixed = jnp.concatenate([fox_ref[...], diff_ref[...]], axis=1)
    a = jnp.dot(mixed, wo_ref[...], preferred_element_type=F32)
    h1 = _layer_norm(dn_alpha * h_ref[...] + a, g_ref[...], b_ref[...])
    h1_ref[...] = h1
    x_hi = h1.astype(BF16)
    h1b_ref[...] = x_hi
    x_lo = (h1 - x_hi.astype(F32)).astype(BF16)
    r1 = jnp.dot(x_hi, wr_ref[...], preferred_element_type=F32)
    r2 = jnp.dot(x_lo, wr_ref[...], preferred_element_type=F32)
    logits = r1 + pltpu.roll(r1, 112, 1) + r2 + br_ref[...]
    lane = lax.broadcasted_iota(jnp.int32, logits.shape, 1)
    comb_ref[...] = _route(logits, lane)


def _out_proj(fox, diff, h, wo, g, b, wr, br, *, tm, dn_alpha):
    n, d = h.shape
    row = lambda i: (i, 0)
    const = lambda i: (0, 0)
    return pl.pallas_call(
        functools.partial(_outproj_kernel, dn_alpha=dn_alpha),
        grid=(n // tm,),
        in_specs=[
            pl.BlockSpec((tm, 512), row), pl.BlockSpec((tm, 512), row), pl.BlockSpec((tm, d), row),
            pl.BlockSpec(wo.shape, const), pl.BlockSpec((1, d), const), pl.BlockSpec((1, d), const),
            pl.BlockSpec(wr.shape, const), pl.BlockSpec((1, LANES), const),
        ],
        out_specs=(pl.BlockSpec((tm, d), row), pl.BlockSpec((tm, d), row), pl.BlockSpec((tm, LANES), row)),
        out_shape=(jax.ShapeDtypeStruct((n, d), F32), jax.ShapeDtypeStruct((n, d), BF16),
                   jax.ShapeDtypeStruct((n, LANES), F32)),
        compiler_params=pltpu.CompilerParams(dimension_semantics=("arbitrary",),
                                             vmem_limit_bytes=VMEM_LIMIT),
        name="out_proj_ln_router",
    )(fox, diff, h, wo, g, b, wr, br)


def _moe_kernel(x_ref, h1_ref, comb_ref, wg_ref, wu_ref, wd_ref, g_ref, b_ref, o_ref, acc_ref,
                *, dn_alpha):
    e = pl.program_id(1)

    @pl.when(e == 0)
    def _():
        acc_ref[...] = jnp.zeros_like(acc_ref)

    x = x_ref[...]
    hg = jnp.dot(x, wg_ref[...], preferred_element_type=F32)
    hu = jnp.dot(x, wu_ref[...], preferred_element_type=F32)
    comb = comb_ref[...]
    lane = lax.broadcasted_iota(jnp.int32, comb.shape, 1)
    ce = jnp.sum(jnp.where(lane == e, comb, 0.0), axis=1, keepdims=True)
    act = hg * jax.nn.sigmoid(hg) * hu * ce
    acc_ref[...] += jnp.dot(act.astype(BF16), wd_ref[...], preferred_element_type=F32)

    @pl.when(e == N_EXPERTS - 1)
    def _():
        o_ref[...] = _layer_norm(dn_alpha * h1_ref[...] + acc_ref[...], g_ref[...], b_ref[...])


def _moe(xb, h1, comb, wg, wu, wd, g, b, *, tm, dn_alpha):
    n, d = h1.shape
    f = wg.shape[-1]
    row = lambda i, e: (i, 0)
    const = lambda i, e: (0, 0)
    return pl.pallas_call(
        functools.partial(_moe_kernel, dn_alpha=dn_alpha),
        grid=(n // tm, N_EXPERTS),
        in_specs=[
            pl.BlockSpec((tm, d), row), pl.BlockSpec((tm, d), row), pl.BlockSpec((tm, LANES), row),
            pl.BlockSpec((None, d, f), lambda i, e: (e, 0, 0)),
            pl.BlockSpec((None, d, f), lambda i, e: (e, 0, 0)),
            pl.BlockSpec((None, f, d), lambda i, e: (e, 0, 0)),
            pl.BlockSpec((1, d), const), pl.BlockSpec((1, d), const),
        ],
        out_specs=pl.BlockSpec((tm, d), row),
        out_shape=jax.ShapeDtypeStruct((n, d), F32),
        scratch_shapes=[pltpu.VMEM((tm, d), F32)],
        compiler_params=pltpu.CompilerParams(dimension_semantics=("arbitrary", "arbitrary"),
                                             vmem_limit_bytes=VMEM_LIMIT),
        name="moe_ln",
    )(xb, h1, comb, wg, wu, wd, g, b)


def _proj_sample_kernel(x_ref, wm_ref, wf_ref, bf_ref, p_ref, lf_ref):
    xb = x_ref[...].astype(BF16)
    p = jnp.dot(xb, wm_ref[...], preferred_element_type=F32)
    col = lax.broadcasted_iota(jnp.int32, p.shape, 1)
    is_q = (col < 512) | ((col >= 1536) & (col < 2048))
    p_ref[...] = jnp.where(is_q, p * QK_SCALE, p)
    z = jnp.dot(xb, wf_ref[...], preferred_element_type=F32) + bf_ref[...]
    lf_ref[...] = jnp.minimum(z, 0.0) - jnp.log1p(jnp.exp(-jnp.abs(z)))


def _project_sample(x, wm, wf, bfv):
    n = x.shape[0]
    return pl.pallas_call(
        _proj_sample_kernel,
        out_shape=(jax.ShapeDtypeStruct((n, wm.shape[1]), F32), jax.ShapeDtypeStruct((n, LANES), F32)),
        compiler_params=pltpu.CompilerParams(vmem_limit_bytes=VMEM_LIMIT),
        name="proj_sample",
    )(x, wm, wf, bfv)


def _past_bias_kernel(pt_ref, lf_hbm, u1_ref, u1f_ref, a_ref, o_ref, buf, sem, *, n_pages):
    layer = pl.program_id(0)
    b = pl.program_id(1)
    copies = [pltpu.make_async_copy(lf_hbm.at[layer, pt_ref[b, j]], buf.at[pl.ds(8 * j, 8)], sem)
              for j in range(n_pages)]
    for cp in copies:
        cp.start()
    for cp in copies:
        cp.wait()
    parts = [t.astype(BF16) for t in _split3(buf[...] * LOG2E)]
    u1 = u1_ref[...]
    u1f = u1f_ref[...]
    within = sum(jnp.dot(t, u1, preferred_element_type=F32) for t in parts)
    row_tot = sum(jnp.dot(t, u1f, preferred_element_type=F32) for t in parts)
    a = a_ref[...]
    later = sum(jnp.dot(a, t.astype(BF16), preferred_element_type=F32) for t in _split3(row_tot))
    o_ref[...] = within + later


def _past_bias(page_table, lf_flat):
    depth = lf_flat.shape[0]
    nseq, n_pages = page_table.shape
    rows = 8 * n_pages
    l = jnp.arange(LANES)
    same_head = (l[:, None] % 8) == (l[None, :] % 8)
    u1 = (same_head & (l[:, None] // 8 > l[None, :] // 8)).astype(BF16)
    u1f = same_head.astype(BF16)
    r = jnp.arange(rows)
    a = (r[None, :] > r[:, None]).astype(BF16)
    const = lambda d, b, pt: (0, 0)
    return pl.pallas_call(
        functools.partial(_past_bias_kernel, n_pages=n_pages),
        grid_spec=pltpu.PrefetchScalarGridSpec(
            num_scalar_prefetch=1, grid=(depth, nseq),
            in_specs=[pl.BlockSpec(memory_space=pl.ANY),
                      pl.BlockSpec((LANES, LANES), const), pl.BlockSpec((LANES, LANES), const),
                      pl.BlockSpec((rows, rows), const)],
            out_specs=pl.BlockSpec((None, None, rows, LANES), lambda d, b, pt: (d, b, 0, 0)),
            scratch_shapes=[pltpu.VMEM((rows, LANES), F32), pltpu.SemaphoreType.DMA(())]),
        out_shape=jax.ShapeDtypeStruct((depth, nseq, rows, LANES), F32),
        compiler_params=pltpu.CompilerParams(dimension_semantics=("arbitrary", "arbitrary"),
                                             vmem_limit_bytes=VMEM_LIMIT),
        name="decode_past_bias",
    )(page_table, lf_flat, u1, u1f, a)


def _decode_softmax_step(s, m_ref, l_ref, acc_ref, v):
    m_prev = m_ref[...]
    m_new = jnp.maximum(m_prev, jnp.max(s, axis=1, keepdims=True))
    p = jnp.exp2(s - m_new)
    alpha = jnp.exp2(m_prev - m_new)
    l_ref[...] = alpha * l_ref[...] + jnp.sum(p, axis=1, keepdims=True)
    acc_ref[...] = alpha * acc_ref[...] + jnp.dot(p, v, preferred_element_type=F32)
    m_ref[...] = m_new


def _decode_finish(q, k_new, v_new, m_ref, l_ref, acc_ref):
    s_self = jnp.sum(q * k_new, axis=1, keepdims=True)
    m_prev = m_ref[...]
    m_fin = jnp.maximum(m_prev, s_self)
    alpha = jnp.exp2(m_prev - m_fin)
    p_self = jnp.exp2(s_self - m_fin)
    l_fin = alpha * l_ref[...] + p_self
    return (alpha * acc_ref[...] + p_self * v_new) / l_fin


def _decode_attn_kernel(pt_ref, fq_ref, fkn_ref, fvn_ref, lfn_ref, bias_ref, fkc_ref, fvc_ref,
                        dq_ref, dkn_ref, dvn_ref, dkc_ref, dvc_ref,
                        lq1_ref, lk1_ref, lq2_ref, lk2_ref, g_ref,
                        fo_ref, do_ref, mf_ref, lf_ref, accf_ref, md_ref, ld_ref, accd_ref,
                        *, page, past_len, lambda_init):
    j = pl.program_id(1)

    @pl.when(j == 0)
    def _():
        mf_ref[...] = jnp.full(mf_ref.shape, NEG, F32)
        md_ref[...] = jnp.full(md_ref.shape, NEG, F32)
        lf_ref[...] = jnp.zeros_like(lf_ref)
        ld_ref[...] = jnp.zeros_like(ld_ref)
        accf_ref[...] = jnp.zeros_like(accf_ref)
        accd_ref[...] = jnp.zeros_like(accd_ref)

    fq = fq_ref[...]
    nf = page * N_FOX_HEADS
    s = _nt_dot(fq, fkc_ref[...])
    bias = bias_ref[...] + lfn_ref[...]
    bias = jnp.concatenate([jnp.broadcast_to(bias[r:r + 1, :], (N_FOX_HEADS, LANES))
                            for r in range(nf // LANES)], axis=1)
    row = lax.broadcasted_iota(jnp.int32, (N_FOX_HEADS, nf), 0)
    col = lax.broadcasted_iota(jnp.int32, (N_FOX_HEADS, nf), 1)
    s = jnp.where((col & (N_FOX_HEADS - 1)) == row, s + bias, NEG)
    _decode_softmax_step(s, mf_ref, lf_ref, accf_ref, fvc_ref[...])

    dq4 = dq_ref[...]
    lane4 = lax.broadcasted_iota(jnp.int32, dq4.shape, 1)
    dq8 = jnp.concatenate([jnp.where(lane4 < HEAD_DIM, dq4, 0.0), jnp.where(lane4 >= HEAD_DIM, dq4, 0.0)], axis=0)
    nd = page * N_DIFF_HEADS
    sd = _nt_dot(dq8, dkc_ref[...])
    rowd = lax.broadcasted_iota(jnp.int32, (8, nd), 0)
    cold = lax.broadcasted_iota(jnp.int32, (8, nd), 1)
    hrow = rowd & (N_DIFF_HEADS - 1)
    slope = jnp.exp2(-(8.0 / N_DIFF_HEADS) * (hrow.astype(F32) + 1.0))
    dist = (past_len - j * page - (cold >> 2)).astype(F32)
    sd = jnp.where((cold & (N_DIFF_HEADS - 1)) == hrow, sd - (slope * LOG2E) * dist, NEG)
    _decode_softmax_step(sd, md_ref, ld_ref, accd_ref, dvc_ref[...])

    @pl.when(j == pl.num_programs(1) - 1)
    def _():
        fo_ref[...] = _decode_finish(fq, fkn_ref[...], fvn_ref[...], mf_ref, lf_ref, accf_ref)
        dk8 = jnp.concatenate([dkn_ref[...], dkn_ref[...]], axis=0)
        dv8 = jnp.concatenate([dvn_ref[...], dvn_ref[...]], axis=0)
        od = _decode_finish(dq8, dk8, dv8, md_ref, ld_ref, accd_ref)
        lam = _diff_lambda(lq1_ref[...], lk1_ref[...], lq2_ref[...], lk2_ref[...], lambda_init)
        do_ref[...] = _diff_combine(od[0:N_DIFF_HEADS], od[N_DIFF_HEADS:], lam, g_ref[...], lambda_init)


def _decode_attention(page_table, layer, fq, fkn, fvn, lfn, bias, fkc, fvc, dq, dkn, dvn, dkc, dvc,
                      lq1, lk1, lq2, lk2, g, *, lambda_init):
    nseq, n_pages = page_table.shape
    page = fkc.shape[2] // N_FOX_HEADS
    seq3 = lambda b, j, pt: (b, 0, 0)
    vec = lambda b, j, pt: (0, 0)
    pool = lambda b, j, pt: (layer, pt[b, j], 0, 0)
    in_specs = [
        pl.BlockSpec((None, N_FOX_HEADS, HEAD_DIM), seq3),
        pl.BlockSpec((None, N_FOX_HEADS, HEAD_DIM), seq3),
        pl.BlockSpec((None, N_FOX_HEADS, HEAD_DIM), seq3),
        pl.BlockSpec((None, 1, LANES), seq3),
        pl.BlockSpec((None, None, 8, LANES), lambda b, j, pt: (b, j, 0, 0)),
        pl.BlockSpec((None, None) + fkc.shape[2:], pool),
        pl.BlockSpec((None, None) + fvc.shape[2:], pool),
        pl.BlockSpec((None, N_DIFF_HEADS, LANES), seq3),
        pl.BlockSpec((None, N_DIFF_HEADS, LANES), seq3),
        pl.BlockSpec((None, N_DIFF_HEADS, LANES), seq3),
        pl.BlockSpec((None, None) + dkc.shape[2:], pool),
        pl.BlockSpec((None, None) + dvc.shape[2:], pool),
        pl.BlockSpec((1, HEAD_DIM), vec), pl.BlockSpec((1, HEAD_DIM), vec),
        pl.BlockSpec((1, HEAD_DIM), vec), pl.BlockSpec((1, HEAD_DIM), vec),
        pl.BlockSpec((1, LANES), vec),
    ]
    return pl.pallas_call(
        functools.partial(_decode_attn_kernel, page=page, past_len=n_pages * page, lambda_init=lambda_init),
        grid_spec=pltpu.PrefetchScalarGridSpec(
            num_scalar_prefetch=1, grid=(nseq, n_pages), in_specs=in_specs,
            out_specs=(pl.BlockSpec((None, N_FOX_HEADS, HEAD_DIM), seq3),
                       pl.BlockSpec((None, N_DIFF_HEADS, LANES), seq3)),
            scratch_shapes=[pltpu.VMEM((8, 1), F32), pltpu.VMEM((8, 1), F32), pltpu.VMEM((8, HEAD_DIM), F32),
                            pltpu.VMEM((8, 1), F32), pltpu.VMEM((8, 1), F32), pltpu.VMEM((8, LANES), F32)]),
        out_shape=(jax.ShapeDtypeStruct((nseq, N_FOX_HEADS, HEAD_DIM), F32),
                   jax.ShapeDtypeStruct((nseq, N_DIFF_HEADS, LANES), F32)),
        compiler_params=pltpu.CompilerParams(dimension_semantics=("arbitrary", "arbitrary"),
                                             vmem_limit_bytes=VMEM_LIMIT),
        name="decode_attn",
    )(page_table, fq, fkn, fvn, lfn, bias, fkc, fvc, dq, dkn, dvn, dkc, dvc, lq1, lk1, lq2, lk2, g)


def _pad_lanes(x, width=LANES):
    return jnp.pad(x, [(0, 0)] * (x.ndim - 1) + [(0, width - x.shape[-1])])


def _alibi_tables(seq_len):
    pos = jnp.arange(seq_len, dtype=F32)
    slopes = jnp.asarray([2.0 ** (-8.0 * (i + 1) / N_DIFF_HEADS) for i in range(N_DIFF_HEADS)], F32)
    u = jnp.stack(_split3(slopes[:, None] * LOG2E * pos[None, :]), axis=-1)
    one = jnp.ones_like(u)
    qb = _pad_lanes(jnp.concatenate([-u, one], axis=-1)).astype(BF16)
    kb = _pad_lanes(jnp.concatenate([one, u], axis=-1)).astype(BF16)
    return qb, kb


def kernel(x_prompt, x_sample, cache_fox_k, cache_fox_v, cache_fox_logf, cache_diff_k, cache_diff_v,
           page_table, w_in, b_forget, w_out, lambda_q1, lambda_k1, lambda_q2, lambda_k2, subln_g,
           ln1_g, ln1_b, ln2_g, ln2_b, w_router, b_router, w_gate, w_up, w_down):
    batch, seq_len, d_model = x_prompt.shape
    depth = w_in.shape[0]
    nseq = x_sample.shape[0]
    n_pool, page = cache_fox_k.shape[1], cache_fox_k.shape[2]
    n = batch * seq_len
    dn_alpha = (2 * depth) ** 0.25
    blk = min(512, seq_len)
    tm_proj = min(256, seq_len)
    tm_tok = min(512, n)
    tm_moe = min(1024, n)

    gate0 = 3 * FOX_WIDTH
    wm = jnp.concatenate([w_in[:, :, :gate0], w_in[:, :, gate0 + N_FOX_HEADS:]], axis=2).astype(BF16)
    wf = _pad_lanes(jnp.tile(w_in[:, :, gate0:gate0 + N_FOX_HEADS], (1, 1, 3))).astype(BF16)
    bfv = _pad_lanes(jnp.tile(b_forget, (1, 3)))[:, None, :]
    wo = w_out.astype(BF16)
    wr_hi = _round8(w_router)
    wr_lo = _round8(w_router - wr_hi)
    wr = _pad_lanes(jnp.concatenate([wr_hi, wr_lo], axis=1)).astype(BF16)
    br = _pad_lanes(b_router[None, :])
    wg = w_gate.astype(BF16)
    wu = w_up.astype(BF16)
    wd = w_down.astype(BF16)
    tri = jnp.tril(jnp.ones((tm_proj, tm_proj), BF16))
    dqb, dkb = _alibi_tables(seq_len)

    fkc = cache_fox_k.reshape(depth, n_pool, page * N_FOX_HEADS, HEAD_DIM)
    fvc = cache_fox_v.reshape(depth, n_pool, page * N_FOX_HEADS, HEAD_DIM)
    dkc = cache_diff_k.reshape(depth, n_pool, page * N_DIFF_HEADS, 2 * HEAD_DIM)
    dvc = cache_diff_v.reshape(depth, n_pool, page * N_DIFF_HEADS, 2 * HEAD_DIM)
    lf_flat = cache_fox_logf.reshape(depth, n_pool, page * N_FOX_HEADS // LANES, LANES)
    past_bias = _past_bias(page_table, lf_flat)
    past_bias = past_bias.reshape(depth, nseq, page_table.shape[1], 8, LANES)

    hp = x_prompt.reshape(n, d_model)
    hs = x_sample.reshape(nseq, d_model)
    p_rows = [[] for _ in range(5)]
    s_rows = [[] for _ in range(5)]
    for layer in range(depth):
        lambda_init = 0.8 - 0.6 * math.exp(-0.3 * layer)
        lam_args = (lambda_q1[layer][None], lambda_k1[layer][None], lambda_q2[layer][None],
                    lambda_k2[layer][None], subln_g[layer][None])
        ln1 = (ln1_g[layer][None], ln1_b[layer][None])
        ln2 = (ln2_g[layer][None], ln2_b[layer][None])

        (fq, fkaug, fvaug, cqb, dq, dkaug, dvaug, fk, fv, lf, dk, dv) = _project_prompt(
            hp, wm[layer], wf[layer], bfv[layer], tri, dkb, seq_len=seq_len, tm=tm_proj)
        fox = _fox_attention(fq, cqb, fkaug, fvaug, batch=batch, seq_len=seq_len, blk=blk)
        diff = _diff_attention(dq, dqb, dkaug, dvaug, *lam_args, batch=batch, seq_len=seq_len, blk=blk,
                               lambda_init=lambda_init)
        h1, h1b, comb = _out_proj(fox, diff, hp, wo[layer], *ln1, wr, br, tm=tm_tok, dn_alpha=dn_alpha)
        hp = _moe(h1b, h1, comb, wg[layer], wu[layer], wd[layer], *ln2, tm=tm_moe, dn_alpha=dn_alpha)
        for lst, val in zip(p_rows, (
                fk.reshape(batch, seq_len, N_FOX_HEADS, HEAD_DIM), fv.reshape(batch, seq_len, N_FOX_HEADS, HEAD_DIM),
                lf.reshape(batch, seq_len, N_FOX_HEADS),
                dk.reshape(batch, seq_len, N_DIFF_HEADS, 2 * HEAD_DIM),
                dv.reshape(batch, seq_len, N_DIFF_HEADS, 2 * HEAD_DIM))):
            lst.append(val)

        ps, lfs = _project_sample(hs, wm[layer], wf[layer], bfv[layer])
        sfq, sfk, sfv, sdq, sdk, sdv = (ps[:, 512 * c:512 * (c + 1)] for c in range(6))
        lfs = lfs[:, :N_FOX_HEADS]
        lfn = jnp.tile(lfs * LOG2E, (1, LANES // N_FOX_HEADS))[:, None, :]
        f3 = lambda a: a.reshape(nseq, N_FOX_HEADS, HEAD_DIM)
        d3 = lambda a: a.reshape(nseq, N_DIFF_HEADS, 2 * HEAD_DIM)
        fo, do = _decode_attention(page_table, layer, f3(sfq), f3(sfk), f3(sfv), lfn, past_bias[layer],
                                   fkc, fvc, d3(sdq), d3(sdk), d3(sdv), dkc, dvc, *lam_args,
                                   lambda_init=lambda_init)
        sfox = fo.reshape(nseq, FOX_WIDTH).astype(BF16)
        sdiff = do.reshape(nseq, DIFF_WIDTH).astype(BF16)
        s1, s1b, scomb = _out_proj(sfox, sdiff, hs, wo[layer], *ln1, wr, br, tm=nseq, dn_alpha=dn_alpha)
        hs = _moe(s1b, s1, scomb, wg[layer], wu[layer], wd[layer], *ln2, tm=nseq, dn_alpha=dn_alpha)
        for lst, val in zip(s_rows, (
                sfk.reshape(nseq, 1, N_FOX_HEADS, HEAD_DIM), sfv.reshape(nseq, 1, N_FOX_HEADS, HEAD_DIM),
                lfs.reshape(nseq, 1, N_FOX_HEADS),
                sdk.reshape(nseq, 1, N_DIFF_HEADS, 2 * HEAD_DIM), sdv.reshape(nseq, 1, N_DIFF_HEADS, 2 * HEAD_DIM))):
            lst.append(val)

    outs_p = [jnp.stack(r, 0) for r in p_rows]
    outs_s = [jnp.stack(r, 0) for r in s_rows]
    return (hp.reshape(batch, seq_len, d_model), hs.reshape(nseq, 1, d_model), *outs_p, *outs_s)
```

```python
import functools
import math

import jax
import jax.numpy as jnp
from jax import lax
from jax.experimental import pallas as pl
from jax.experimental.pallas import tpu as pltpu

F32 = jnp.float32
BF16 = jnp.bfloat16

HEAD_DIM = 64
N_FOX_HEADS = 8
N_DIFF_HEADS = 4
FOX_WIDTH = N_FOX_HEADS * HEAD_DIM
DIFF_WIDTH = N_DIFF_HEADS * 2 * HEAD_DIM
N_EXPERTS = 16
N_GROUPS = 4
EXPERTS_PER_GROUP = N_EXPERTS // N_GROUPS
LN_EPS = 1e-5
RMS_EPS = 1e-5
LOG2E = 1.4426950408889634
QK_SCALE = HEAD_DIM ** -0.5 * LOG2E
NEG = -1e30
LANES = 128
VMEM_LIMIT = 56 * 1024 * 1024
DIFF_V_ROWS = LANES + 16


def _round8(x):
    t = x * 65537.0
    return t - (t - x)


def _split3(x):
    hi = _round8(x)
    r = x - hi
    mid = _round8(r)
    return hi, mid, r - mid


def _layer_norm(x, g, b):
    mu = jnp.mean(x, axis=-1, keepdims=True)
    xc = x - mu
    var = jnp.mean(xc * xc, axis=-1, keepdims=True)
    return xc * lax.rsqrt(var + LN_EPS) * g + b


def _nt_dot(a, b):
    return lax.dot_general(a, b, (((1,), (1,)), ((), ())), preferred_element_type=F32)


def _proj_kernel(*refs, blocks_per_seq, n_prev):
    x_ref, wm_ref, wf_ref, bf_ref, tri_ref, dkb_ref = refs[0:6]
    prev = refs[6:6 + 5] if n_prev else ()
    (fqT_ref, fkaug_ref, fvT_ref, cqbT_ref, dqT_ref, dkaug_ref, dvT_ref,
     fk_ref, fv_ref, lf_ref, dk_ref, dv_ref, carry_ref) = refs[6 + len(prev):]
    i = pl.program_id(0)
    for src, dst in zip(prev, (fk_ref, fv_ref, lf_ref, dk_ref, dv_ref)):
        for l in range(n_prev):
            dst[l] = src[l]

    @pl.when(i % blocks_per_seq == 0)
    def _():
        carry_ref[...] = jnp.zeros_like(carry_ref)

    xb = x_ref[...].astype(BF16)
    p = jnp.dot(xb, wm_ref[...], preferred_element_type=F32)
    tm = p.shape[0]
    lane = lax.broadcasted_iota(jnp.int32, (tm, LANES), 1)

    fqT_ref[...] = (p[:, 0:512] * QK_SCALE).T.astype(BF16)
    fk = p[:, 512:1024]
    fv = p[:, 1024:1536]
    dqT_ref[...] = (p[:, 1536:2048] * QK_SCALE).T.astype(BF16)
    dk = p[:, 2048:2560]
    dv = p[:, 2560:3072]
    fvT = fv.T
    dvT = dv.T
    fk_ref[n_prev] = fk.T
    fv_ref[n_prev] = fvT
    dk_ref[n_prev] = dk
    dv_ref[n_prev] = dv

    z = jnp.dot(xb, wf_ref[...], preferred_element_type=F32) + bf_ref[...]
    lf = jnp.minimum(z, 0.0) - jnp.log1p(jnp.exp(-jnp.abs(z)))
    lf = jnp.where(lane < 24, lf, 0.0)
    lf_ref[n_prev] = lf.T[0:N_FOX_HEADS, :]

    hi, mid, lo = _split3(lf)
    packed = jnp.where(lane < 8, hi, jnp.where(lane < 16, mid, lo)).astype(BF16)
    zc = jnp.dot(tri_ref[...], packed, preferred_element_type=F32)
    c = zc + pltpu.roll(zc, 120, 1) + pltpu.roll(zc, 112, 1) + carry_ref[...]
    carry_ref[...] = c[tm - 1:tm, :]

    c2 = jnp.where(lane < 8, c * LOG2E, 0.0)
    c2rep = c2 + pltpu.roll(c2, 8, 1) + pltpu.roll(c2, 16, 1)
    h3, m3, l3 = _split3(c2rep)
    parts = jnp.where(lane < 8, h3, jnp.where(lane < 16, m3, l3))
    kparts = pltpu.roll(-parts, 32, 1)
    cqb = jnp.where((lane >= 32) & (lane < 56), 1.0, parts)
    kb = jnp.where(lane < 24, 1.0, kparts).astype(BF16)
    cqbT_ref[...] = cqb.T.astype(BF16)

    rowi = lax.broadcasted_iota(jnp.int32, (LANES, tm), 0)
    for j in range(4):
        sl = slice(LANES * j, LANES * (j + 1))
        fkaug_ref[j, :, 0:LANES] = fk[:, sl].astype(BF16)
        fkaug_ref[j, :, LANES:2 * LANES] = kb
        vpT = fvT[sl, :]
        fvT_ref[2 * j, 0] = jnp.where(rowi < HEAD_DIM, vpT, 1.0).astype(BF16)
        fvT_ref[2 * j + 1, 0] = jnp.where(rowi >= HEAD_DIM, vpT, 1.0).astype(BF16)
        dkaug_ref[j, :, 0:LANES] = dk[:, sl].astype(BF16)
        dkaug_ref[j, :, LANES:2 * LANES] = dkb_ref[j]
        dvT_ref[j, 0, 0:LANES, :] = dvT[sl, :].astype(BF16)
        dvT_ref[j, 0, LANES:DIFF_V_ROWS, :] = jnp.ones((DIFF_V_ROWS - LANES, tm), BF16)


def _project_prompt(x, wm, wf, bfv, tri, dkb, prev, *, batch, seq_len, tm):
    n, d = x.shape
    nb = n // tm
    bps = seq_len // tm
    n_prev = 0 if prev is None else prev[0].shape[0]
    nl = n_prev + 1
    row = lambda i: (i, 0)
    row3 = lambda i: (0, i, 0)
    const = lambda i: (0, 0)
    tmaj = lambda i: (0, i // bps, 0, i % bps)

    def stacked_specs(k):
        return [pl.BlockSpec((k, None, FOX_WIDTH, tm), tmaj), pl.BlockSpec((k, None, FOX_WIDTH, tm), tmaj),
                pl.BlockSpec((k, None, N_FOX_HEADS, tm), tmaj),
                pl.BlockSpec((k, tm, DIFF_WIDTH), row3), pl.BlockSpec((k, tm, DIFF_WIDTH), row3)]

    out_shape = (
        jax.ShapeDtypeStruct((512, n), BF16),
        jax.ShapeDtypeStruct((4, n, 256), BF16),
        jax.ShapeDtypeStruct((8, nb, LANES, tm), BF16),
        jax.ShapeDtypeStruct((LANES, n), BF16),
        jax.ShapeDtypeStruct((512, n), BF16),
        jax.ShapeDtypeStruct((4, n, 256), BF16),
        jax.ShapeDtypeStruct((4, nb, DIFF_V_ROWS, tm), BF16),
        jax.ShapeDtypeStruct((nl, batch, FOX_WIDTH, seq_len), F32),
        jax.ShapeDtypeStruct((nl, batch, FOX_WIDTH, seq_len), F32),
        jax.ShapeDtypeStruct((nl, batch, N_FOX_HEADS, seq_len), F32),
        jax.ShapeDtypeStruct((nl, n, DIFF_WIDTH), F32),
        jax.ShapeDtypeStruct((nl, n, DIFF_WIDTH), F32),
    )
    colb = lambda i: (0, i)
    chunk = lambda i: (0, i, 0, 0)
    out_specs = tuple([
        pl.BlockSpec((512, tm), colb),
        pl.BlockSpec((4, tm, 256), row3),
        pl.BlockSpec((8, 1, LANES, tm), chunk),
        pl.BlockSpec((LANES, tm), colb),
        pl.BlockSpec((512, tm), colb),
        pl.BlockSpec((4, tm, 256), row3),
        pl.BlockSpec((4, 1, DIFF_V_ROWS, tm), chunk),
    ] + stacked_specs(nl))
    in_specs = [
        pl.BlockSpec((tm, d), row),
        pl.BlockSpec(wm.shape, const),
        pl.BlockSpec(wf.shape, const),
        pl.BlockSpec(bfv.shape, const),
        pl.BlockSpec(tri.shape, const),
        pl.BlockSpec((4, tm, 128), lambda i: (0, i % bps, 0)),
    ] + (stacked_specs(n_prev) if n_prev else [])
    return pl.pallas_call(
        functools.partial(_proj_kernel, blocks_per_seq=bps, n_prev=n_prev),
        grid=(nb,), in_specs=in_specs, out_specs=out_specs, out_shape=out_shape,
        scratch_shapes=[pltpu.VMEM((1, LANES), F32)],
        compiler_params=pltpu.CompilerParams(dimension_semantics=("arbitrary",),
                                             vmem_limit_bytes=VMEM_LIMIT),
        name="proj_prompt",
    )(x, wm, wf, bfv, tri, dkb, *(prev or ()))


def _online_softmax_step(kb, qaug_ref, m_ref, acc_ref, vts_per_chain, masked):
    chains = range(len(vts_per_chain))
    nsub = len(vts_per_chain[0])
    sub = kb.shape[0] // nsub
    sts = [[jnp.dot(kb[k * sub:(k + 1) * sub], qaug_ref[c], preferred_element_type=F32) for c in chains]
           for k in range(nsub)]
    m_cur = [m_ref[c] for c in chains]
    for k in range(nsub):
        if masked:
            key = lax.broadcasted_iota(jnp.int32, sts[k][0].shape, 0) + k * sub
            qry = lax.broadcasted_iota(jnp.int32, sts[k][0].shape, 1)
            sts[k] = [jnp.where(key <= qry, st, NEG) for st in sts[k]]
        m_new = [jnp.maximum(m_cur[c], jnp.max(sts[k][c], axis=0, keepdims=True)) for c in chains]
        pts = [jnp.exp2(sts[k][c] - m_new[c]).astype(BF16) for c in chains]
        for c in chains:
            new = jnp.dot(vts_per_chain[c][k], pts[c], preferred_element_type=F32)
            acc_ref[c] = jnp.exp2(m_cur[c] - m_new[c]) * acc_ref[c] + new
        m_cur = m_new
    for c in chains:
        m_ref[c] = m_cur[c]


def _causal_sweep(qi, step):
    def body(p, carry):
        step(2 * p, 2, False)
        return carry
    lax.fori_loop(0, qi // 2, body, 0)

    @pl.when(qi % 2 == 1)
    def _():
        step(qi - 1, 1, False)

    step(qi, 1, True)


def _fox_attn_kernel(qT_ref, cqbT_ref, k_ref, vT_ref, o_ref, qaug_ref, acc_ref, m_ref, *, blk, nsub):
    pair = pl.program_id(1)
    qi = pl.program_id(2)
    rowi = lax.broadcasted_iota(jnp.int32, (LANES, blk), 0)
    qT = qT_ref[...].astype(F32)
    cT = cqbT_ref[...].astype(F32)
    bias_rows = (rowi < 24) | ((rowi >= 32) & (rowi < 56))
    for hh in range(2):
        h = 2 * pair + hh
        in_head = (rowi >= HEAD_DIM * hh) & (rowi < HEAD_DIM * (hh + 1))
        sel = ((rowi & 7) == h) & bias_rows
        qaug_ref[hh, 0:LANES, :] = jnp.where(in_head, qT, 0.0).astype(BF16)
        qaug_ref[hh, LANES:2 * LANES, :] = jnp.where(sel, cT, 0.0).astype(BF16)
    m_ref[...] = jnp.full(m_ref.shape, NEG, F32)
    acc_ref[...] = jnp.zeros(acc_ref.shape, F32)

    def step(ki, nblk, masked):
        start = pl.multiple_of(ki * blk, blk)
        kb = k_ref[pl.ds(start, nblk * blk), :]
        vts = [[vT_ref[hh, ki * nsub + c] for c in range(nblk * nsub)] for hh in range(2)]
        _online_softmax_step(kb, qaug_ref, m_ref, acc_ref, vts, masked)

    _causal_sweep(qi, step)
    a0 = acc_ref[0]
    a1 = acc_ref[1]
    o0 = a0 / a0[HEAD_DIM:HEAD_DIM + 1, :]
    o1 = a1 / a1[0:1, :]
    o_ref[...] = jnp.where(rowi < HEAD_DIM, o0, o1).T.astype(o_ref.dtype)


def _fox_attention(fqT, cqbT, fkaug, fvT, *, batch, seq_len, blk):
    n = fqT.shape[1]
    nq = seq_len // blk
    sub = fvT.shape[-1]
    return pl.pallas_call(
        functools.partial(_fox_attn_kernel, blk=blk, nsub=blk // sub),
        grid=(batch, 4, nq),
        in_specs=[
            pl.BlockSpec((LANES, blk), lambda b, p, i: (p, b * nq + i)),
            pl.BlockSpec((LANES, blk), lambda b, p, i: (0, b * nq + i)),
            pl.BlockSpec((None, seq_len, 256), lambda b, p, i: (p, b, 0)),
            pl.BlockSpec((2, seq_len // sub, LANES, sub), lambda b, p, i: (p, b, 0, 0)),
        ],
        out_specs=pl.BlockSpec((blk, LANES), lambda b, p, i: (b * nq + i, p)),
        out_shape=jax.ShapeDtypeStruct((n, FOX_WIDTH), BF16),
        scratch_shapes=[pltpu.VMEM((2, 256, blk), BF16), pltpu.VMEM((2, LANES, blk), F32),
                        pltpu.VMEM((2, 1, blk), F32)],
        compiler_params=pltpu.CompilerParams(
            dimension_semantics=("arbitrary", "arbitrary", "arbitrary"), vmem_limit_bytes=VMEM_LIMIT),
        name="fox_attn",
    )(fqT, cqbT, fkaug, fvT)


def _diff_lambda(lq1, lk1, lq2, lk2, lambda_init):
    return (jnp.exp(jnp.sum(lq1 * lk1, axis=1, keepdims=True))
            - jnp.exp(jnp.sum(lq2 * lk2, axis=1, keepdims=True)) + lambda_init)


def _diff_combine(o1, o2, lam, g, lambda_init, axis):
    a = o1 - lam * o2
    a = a * lax.rsqrt(jnp.mean(a * a, axis=axis, keepdims=True) + RMS_EPS)
    return a * g * (1.0 - lambda_init)


def _diff_attn_kernel(qT_ref, qbT_ref, k_ref, vT_ref, lq1_ref, lk1_ref, lq2_ref, lk2_ref, g_ref,
                      o_ref, qaug_ref, acc_ref, m_ref, *, blk, nsub, lambda_init):
    qi = pl.program_id(2)
    rowi = lax.broadcasted_iota(jnp.int32, (LANES, blk), 0)
    qT = qT_ref[...].astype(F32)
    for c in range(2):
        half = (rowi >= HEAD_DIM * c) & (rowi < HEAD_DIM * (c + 1))
        qaug_ref[c, 0:LANES, :] = jnp.where(half, qT, 0.0).astype(BF16)
        qaug_ref[c, LANES:2 * LANES, :] = qbT_ref[...]
    m_ref[...] = jnp.full(m_ref.shape, NEG, F32)
    acc_ref[...] = jnp.zeros(acc_ref.shape, F32)

    def step(ki, nblk, masked):
        start = pl.multiple_of(ki * blk, blk)
        kb = k_ref[pl.ds(start, nblk * blk), :]
        vts = [vT_ref[ki * nsub + c] for c in range(nblk * nsub)]
        _online_softmax_step(kb, qaug_ref, m_ref, acc_ref, [vts, vts], masked)

    _causal_sweep(qi, step)
    a1 = acc_ref[0]
    a2 = acc_ref[1]
    o1 = a1[0:LANES] / a1[LANES:LANES + 1]
    o2 = a2[0:LANES] / a2[LANES:LANES + 1]
    lam = _diff_lambda(lq1_ref[...], lk1_ref[...], lq2_ref[...], lk2_ref[...], lambda_init)
    o_ref[...] = _diff_combine(o1, o2, lam, g_ref[...], lambda_init, 0).T.astype(o_ref.dtype)


def _diff_attention(dqT, dqbT, dkaug, dvT, lq1, lk1, lq2, lk2, gcol, *, batch, seq_len, blk, lambda_init):
    n = dqT.shape[1]
    nq = seq_len // blk
    sub = dvT.shape[-1]
    vrows = dvT.shape[-2]
    vec = lambda b, h, i: (0, 0)
    return pl.pallas_call(
        functools.partial(_diff_attn_kernel, blk=blk, nsub=blk // sub, lambda_init=lambda_init),
        grid=(batch, N_DIFF_HEADS, nq),
        in_specs=[
            pl.BlockSpec((LANES, blk), lambda b, h, i: (h, b * nq + i)),
            pl.BlockSpec((None, LANES, blk), lambda b, h, i: (h, 0, i)),
            pl.BlockSpec((None, seq_len, 256), lambda b, h, i: (h, b, 0)),
            pl.BlockSpec((None, seq_len // sub, vrows, sub), lambda b, h, i: (h, b, 0, 0)),
            pl.BlockSpec((1, HEAD_DIM), vec), pl.BlockSpec((1, HEAD_DIM), vec),
            pl.BlockSpec((1, HEAD_DIM), vec), pl.BlockSpec((1, HEAD_DIM), vec),
            pl.BlockSpec((LANES, 1), vec),
        ],
        out_specs=pl.BlockSpec((blk, LANES), lambda b, h, i: (b * nq + i, h)),
        out_shape=jax.ShapeDtypeStruct((n, DIFF_WIDTH), BF16),
        scratch_shapes=[pltpu.VMEM((2, 256, blk), BF16), pltpu.VMEM((2, vrows, blk), F32),
                        pltpu.VMEM((2, 1, blk), F32)],
        compiler_params=pltpu.CompilerParams(
            dimension_semantics=("arbitrary", "arbitrary", "arbitrary"), vmem_limit_bytes=VMEM_LIMIT),
        name="diff_attn",
    )(dqT, dqbT, dkaug, dvT, lq1, lk1, lq2, lk2, gcol)


def _route(logits, lane):
    lanef = lane.astype(F32)
    lg = jnp.where(lane < N_EXPERTS, logits, NEG)
    mx = jnp.max(lg, axis=1, keepdims=True)
    e = jnp.exp(lg - mx)
    scores = e / jnp.sum(e, axis=1, keepdims=True)
    best = None
    for g in range(N_GROUPS):
        ing = (lane >= EXPERTS_PER_GROUP * g) & (lane < EXPERTS_PER_GROUP * (g + 1))
        sg = jnp.where(ing, scores, -1.0)
        m1 = jnp.max(sg, axis=1, keepdims=True)
        i1 = jnp.min(jnp.where(sg == m1, lanef, 999.0), axis=1, keepdims=True)
        sg2 = jnp.where(lanef == i1, -1.0, sg)
        m2 = jnp.max(sg2, axis=1, keepdims=True)
        i2 = jnp.min(jnp.where(sg2 == m2, lanef, 999.0), axis=1, keepdims=True)
        cand = (m1 + m2, m1, m2, i1, i2)
        if best is None:
            best = cand
        else:
            upd = cand[0] > best[0]
            best = tuple(jnp.where(upd, cn, bs) for cn, bs in zip(cand, best))
    _, v1, v2, i1, i2 = best
    den = v1 + v2
    return jnp.where(lanef == i1, v1 / den, 0.0) + jnp.where(lanef == i2, v2 / den, 0.0)


def _outproj_kernel(fox_ref, diff_ref, h_ref, wo_ref, g_ref, b_ref, wr_ref, br_ref,
                    h1_ref, h1b_ref, comb_ref, *, dn_alpha):
    mixed = jnp.concatenate([fox_ref[...], diff_ref[...]], axis=1)
    a = jnp.dot(mixed, wo_ref[...], preferred_element_type=F32)
    h1 = _layer_norm(dn_alpha * h_ref[...] + a, g_ref[...], b_ref[...])
    h1_ref[...] = h1
    x_hi = h1.astype(BF16)
    h1b_ref[...] = x_hi
    x_lo = (h1 - x_hi.astype(F32)).astype(BF16)
    r1 = jnp.dot(x_hi, wr_ref[...], preferred_element_type=F32)
    r2 = jnp.dot(x_lo, wr_ref[...], preferred_element_type=F32)
    logits = r1 + pltpu.roll(r1, 112, 1) + r2 + br_ref[...]
    lane = lax.broadcasted_iota(jnp.int32, logits.shape, 1)
    comb_ref[...] = _route(logits, lane)


def _out_proj(fox, diff, h, wo, g, b, wr, br, *, tm, dn_alpha):
    n, d = h.shape
    row = lambda i: (i, 0)
    const = lambda i: (0, 0)
    return pl.pallas_call(
        functools.partial(_outproj_kernel, dn_alpha=dn_alpha),
        grid=(n // tm,),
        in_specs=[
            pl.BlockSpec((tm, 512), row), pl.BlockSpec((tm, 512), row), pl.BlockSpec((tm, d), row),
            pl.BlockSpec(wo.shape, const), pl.BlockSpec((1, d), const), pl.BlockSpec((1, d), const),
            pl.BlockSpec(wr.shape, const), pl.BlockSpec((1, LANES), const),
        ],
        out_specs=(pl.BlockSpec((tm, d), row), pl.BlockSpec((tm, d), row), pl.BlockSpec((tm, LANES), row)),
        out_shape=(jax.ShapeDtypeStruct((n, d), F32), jax.ShapeDtypeStruct((n, d), BF16),
                   jax.ShapeDtypeStruct((n, LANES), F32)),
        compiler_params=pltpu.CompilerParams(dimension_semantics=("arbitrary",),
                                             vmem_limit_bytes=VMEM_LIMIT),
        name="out_proj_ln_router",
    )(fox, diff, h, wo, g, b, wr, br)


def _moe_kernel(x_ref, h1_ref, comb_ref, wg_ref, wu_ref, wd_ref, g_ref, b_ref, o_ref, acc_ref,
                *, dn_alpha):
    e = pl.program_id(1)

    @pl.when(e == 0)
    def _():
        acc_ref[...] = jnp.zeros_like(acc_ref)

    x = x_ref[...]
    hg = jnp.dot(x, wg_ref[...], preferred_element_type=F32)
    hu = jnp.dot(x, wu_ref[...], preferred_element_type=F32)
    comb = comb_ref[...]
    lane = lax.broadcasted_iota(jnp.int32, comb.shape, 1)
    ce = jnp.sum(jnp.where(lane == e, comb, 0.0), axis=1, keepdims=True)
    act = hg * jax.nn.sigmoid(hg) * hu * ce
    acc_ref[...] += jnp.dot(act.astype(BF16), wd_ref[...], preferred_element_type=F32)

    @pl.when(e == N_EXPERTS - 1)
    def _():
        o_ref[...] = _layer_norm(dn_alpha * h1_ref[...] + acc_ref[...], g_ref[...], b_ref[...])


def _moe(xb, h1, comb, wg, wu, wd, g, b, *, tm, dn_alpha):
    n, d = h1.shape
    f = wg.shape[-1]
    row = lambda i, e: (i, 0)
    const = lambda i, e: (0, 0)
    return pl.pallas_call(
        functools.partial(_moe_kernel, dn_alpha=dn_alpha),
        grid=(n // tm, N_EXPERTS),
        in_specs=[
            pl.BlockSpec((tm, d), row), pl.BlockSpec((tm, d), row), pl.BlockSpec((tm, LANES), row),
            pl.BlockSpec((None, d, f), lambda i, e: (e, 0, 0)),
            pl.BlockSpec((None, d, f), lambda i, e: (e, 0, 0)),
            pl.BlockSpec((None, f, d), lambda i, e: (e, 0, 0)),
            pl.BlockSpec((1, d), const), pl.BlockSpec((1, d), const),
        ],
        out_specs=pl.BlockSpec((tm, d), row),
        out_shape=jax.ShapeDtypeStruct((n, d), F32),
        scratch_shapes=[pltpu.VMEM((tm, d), F32)],
        compiler_params=pltpu.CompilerParams(dimension_semantics=("arbitrary", "arbitrary"),
                                             vmem_limit_bytes=VMEM_LIMIT),
        name="moe_ln",
    )(xb, h1, comb, wg, wu, wd, g, b)


def _proj_sample_kernel(x_ref, wm_ref, wf_ref, bf_ref, p_ref, lf_ref):
    xb = x_ref[...].astype(BF16)
    p = jnp.dot(xb, wm_ref[...], preferred_element_type=F32)
    col = lax.broadcasted_iota(jnp.int32, p.shape, 1)
    is_q = (col < 512) | ((col >= 1536) & (col < 2048))
    p_ref[...] = jnp.where(is_q, p * QK_SCALE, p)
    z = jnp.dot(xb, wf_ref[...], preferred_element_type=F32) + bf_ref[...]
    lf_ref[...] = jnp.minimum(z, 0.0) - jnp.log1p(jnp.exp(-jnp.abs(z)))


def _project_sample(x, wm, wf, bfv):
    n = x.shape[0]
    return pl.pallas_call(
        _proj_sample_kernel,
        out_shape=(jax.ShapeDtypeStruct((n, wm.shape[1]), F32), jax.ShapeDtypeStruct((n, LANES), F32)),
        compiler_params=pltpu.CompilerParams(vmem_limit_bytes=VMEM_LIMIT),
        name="proj_sample",
    )(x, wm, wf, bfv)


def _rows_per_head(x, rows):
    return jnp.concatenate([jnp.broadcast_to(x[h:h + 1, :], (rows, x.shape[1])) for h in range(x.shape[0])],
                           axis=0)


def _decode_attn_kernel(pt_ref, *refs, group, page, n_pages, lambda_init):
    g = group
    fkc = refs[0:g]
    fvc = refs[g:2 * g]
    lfc = refs[2 * g:3 * g]
    dkc = refs[3 * g:4 * g]
    dvc = refs[4 * g:5 * g]
    (fqc_ref, fq_ref, fkn_ref, fvnc_ref, lfn_ref, tri_ref, dq_ref, dkn_ref, dvn_ref,
     lq1_ref, lk1_ref, lq2_ref, lk2_ref, g_ref,
     fo_ref, do_ref,
     qb_ref, mf_ref, lf_ref, accf_ref, carry_ref, md_ref, ld_ref, accd_ref) = refs[5 * g:]
    j = pl.program_id(1)
    hd = N_FOX_HEADS * HEAD_DIM

    @pl.when(j == 0)
    def _():
        qb_ref[...] = jnp.broadcast_to(fqc_ref[...], (hd, LANES))
        carry_ref[...] = lfn_ref[...]
        mf_ref[...] = jnp.full(mf_ref.shape, NEG, F32)
        md_ref[...] = jnp.full(md_ref.shape, NEG, F32)
        lf_ref[...] = jnp.zeros_like(lf_ref)
        ld_ref[...] = jnp.zeros_like(ld_ref)
        accf_ref[...] = jnp.zeros_like(accf_ref)
        accd_ref[...] = jnp.zeros_like(accd_ref)

    qb = qb_ref[...]
    tri = tri_ref[...]
    carry = carry_ref[...]
    s_list = []
    for i in range(g):
        s = jnp.sum((fkc[i][...] * qb).reshape(N_FOX_HEADS, HEAD_DIM, LANES), axis=1)
        lfp = lfc[i][...] * LOG2E
        within = sum(jnp.dot(t.astype(BF16), tri, preferred_element_type=F32) for t in _split3(lfp))
        s_list.append(s + within + carry)
        carry = carry + jnp.sum(lfp, axis=1, keepdims=True)
    carry_ref[...] = carry
    m_prev = mf_ref[...]
    m_new = m_prev
    for s in s_list:
        m_new = jnp.maximum(m_new, jnp.max(s, axis=1, keepdims=True))
    alpha = jnp.exp2(m_prev - m_new)
    l_new = alpha * lf_ref[...]
    acc = _rows_per_head(jnp.broadcast_to(alpha, (N_FOX_HEADS, LANES)), HEAD_DIM) * accf_ref[...]
    for i in range(g):
        p = jnp.exp2(s_list[i] - m_new)
        l_new = l_new + jnp.sum(p, axis=1, keepdims=True)
        acc = acc + _rows_per_head(p, HEAD_DIM) * fvc[i][...]
    accf_ref[...] = acc
    lf_ref[...] = l_new
    mf_ref[...] = m_new

    dq4 = dq_ref[...]
    lane4 = lax.broadcasted_iota(jnp.int32, dq4.shape, 1)
    dq8 = jnp.concatenate([jnp.where(lane4 < HEAD_DIM, dq4, 0.0), jnp.where(lane4 >= HEAD_DIM, dq4, 0.0)], axis=0)
    nd = page * N_DIFF_HEADS
    rowd = lax.broadcasted_iota(jnp.int32, (8, nd), 0)
    cold = lax.broadcasted_iota(jnp.int32, (8, nd), 1)
    hrow = rowd & (N_DIFF_HEADS - 1)
    slope2 = jnp.exp2(-(8.0 / N_DIFF_HEADS) * (hrow.astype(F32) + 1.0)) * LOG2E
    keep = (cold & (N_DIFF_HEADS - 1)) == hrow
    t_in_page = cold // N_DIFF_HEADS
    sd_list = []
    for i in range(g):
        first_pos = (n_pages - 1 - (j * g + i)) * page
        dist = (n_pages * page - first_pos - t_in_page).astype(F32)
        sd = _nt_dot(dq8, dkc[i][...])
        sd_list.append(jnp.where(keep, sd - slope2 * dist, NEG))
    m_prev = md_ref[...]
    m_new = m_prev
    for sd in sd_list:
        m_new = jnp.maximum(m_new, jnp.max(sd, axis=1, keepdims=True))
    alpha = jnp.exp2(m_prev - m_new)
    l_new = alpha * ld_ref[...]
    accd = alpha * accd_ref[...]
    for i in range(g):
        p = jnp.exp2(sd_list[i] - m_new)
        l_new = l_new + jnp.sum(p, axis=1, keepdims=True)
        accd = accd + jnp.dot(p, dvc[i][...], preferred_element_type=F32)
    accd_ref[...] = accd
    ld_ref[...] = l_new
    md_ref[...] = m_new

    @pl.when(j == pl.num_programs(1) - 1)
    def _():
        fq = fq_ref[...]
        s_self = jnp.sum(fq * fkn_ref[...], axis=1, keepdims=True)
        m_prev = mf_ref[...]
        m_fin = jnp.maximum(m_prev, s_self)
        a = jnp.exp2(m_prev - m_fin)
        p_self = jnp.exp2(s_self - m_fin)
        l_fin = a * lf_ref[...] + p_self
        num = jnp.sum(accf_ref[...], axis=1, keepdims=True)
        fo_ref[...] = ((_rows_per_head(a, HEAD_DIM) * num + _rows_per_head(p_self, HEAD_DIM) * fvnc_ref[...])
                       / _rows_per_head(l_fin, HEAD_DIM))

        dk8 = jnp.concatenate([dkn_ref[...], dkn_ref[...]], axis=0)
        dv8 = jnp.concatenate([dvn_ref[...], dvn_ref[...]], axis=0)
        s_self = jnp.sum(dq8 * dk8, axis=1, keepdims=True)
        m_prev = md_ref[...]
        m_fin = jnp.maximum(m_prev, s_self)
        a = jnp.exp2(m_prev - m_fin)
        p_self = jnp.exp2(s_self - m_fin)
        od = (a * accd_ref[...] + p_self * dv8) / (a * ld_ref[...] + p_self)
        lam = _diff_lambda(lq1_ref[...], lk1_ref[...], lq2_ref[...], lk2_ref[...], lambda_init)
        do_ref[...] = _diff_combine(od[0:N_DIFF_HEADS], od[N_DIFF_HEADS:], lam, g_ref[...], lambda_init, -1)


def _decode_attention(page_table, layer, fkc, fvc, lfc, dkc, dvc, fq, fkn, fvn, lfn, dq, dkn, dvn,
                      lq1, lk1, lq2, lk2, gain, *, group, lambda_init):
    nseq, n_pages = page_table.shape
    page = fkc.shape[-1]
    hd = N_FOX_HEADS * HEAD_DIM
    seq3 = lambda b, j, pt: (b, 0, 0)
    vec = lambda b, j, pt: (0, 0)

    def pool(i):
        return lambda b, j, pt: (layer, pt[b, n_pages - 1 - (j * group + i)], 0, 0)

    def paged(arr):
        return [pl.BlockSpec((None, None) + arr.shape[2:], pool(i)) for i in range(group)]

    tri = (jnp.arange(page)[:, None] > jnp.arange(page)[None, :]).astype(BF16)
    f3 = lambda a: a.reshape(nseq, N_FOX_HEADS, HEAD_DIM)
    d3 = lambda a: a.reshape(nseq, N_DIFF_HEADS, 2 * HEAD_DIM)
    col = lambda a: a.reshape(nseq, -1, 1)
    in_specs = (paged(fkc) + paged(fvc) + paged(lfc) + paged(dkc) + paged(dvc) + [
        pl.BlockSpec((None, hd, 1), seq3),
        pl.BlockSpec((None, N_FOX_HEADS, HEAD_DIM), seq3),
        pl.BlockSpec((None, N_FOX_HEADS, HEAD_DIM), seq3),
        pl.BlockSpec((None, hd, 1), seq3),
        pl.BlockSpec((None, N_FOX_HEADS, 1), seq3),
        pl.BlockSpec((page, page), vec),
        pl.BlockSpec((None, N_DIFF_HEADS, LANES), seq3),
        pl.BlockSpec((None, N_DIFF_HEADS, LANES), seq3),
        pl.BlockSpec((None, N_DIFF_HEADS, LANES), seq3),
        pl.BlockSpec((1, HEAD_DIM), vec), pl.BlockSpec((1, HEAD_DIM), vec),
        pl.BlockSpec((1, HEAD_DIM), vec), pl.BlockSpec((1, HEAD_DIM), vec),
        pl.BlockSpec((1, LANES), vec),
    ])
    args = ([fkc] * group + [fvc] * group + [lfc] * group + [dkc] * group + [dvc] * group
            + [col(fq), f3(fq), f3(fkn), col(fvn), col(lfn), tri, d3(dq), d3(dkn), d3(dvn),
               lq1, lk1, lq2, lk2, gain])
    fo, do = pl.pallas_call(
        functools.partial(_decode_attn_kernel, group=group, page=page, n_pages=n_pages, lambda_init=lambda_init),
        grid_spec=pltpu.PrefetchScalarGridSpec(
            num_scalar_prefetch=1, grid=(nseq, n_pages // group), in_specs=in_specs,
            out_specs=(pl.BlockSpec((None, hd, 1), seq3),
                       pl.BlockSpec((None, N_DIFF_HEADS, LANES), seq3)),
            scratch_shapes=[pltpu.VMEM((hd, LANES), F32),
                            pltpu.VMEM((8, 1), F32), pltpu.VMEM((8, 1), F32), pltpu.VMEM((hd, LANES), F32),
                            pltpu.VMEM((8, 1), F32),
                            pltpu.VMEM((8, 1), F32), pltpu.VMEM((8, 1), F32), pltpu.VMEM((8, LANES), F32)]),
        out_shape=(jax.ShapeDtypeStruct((nseq, hd, 1), F32),
                   jax.ShapeDtypeStruct((nseq, N_DIFF_HEADS, LANES), F32)),
        compiler_params=pltpu.CompilerParams(dimension_semantics=("arbitrary", "arbitrary"),
                                             vmem_limit_bytes=VMEM_LIMIT),
        name="decode_attn",
    )(page_table, *args)
    return fo.reshape(nseq, hd), do.reshape(nseq, DIFF_WIDTH)


def _pad_lanes(x, width=LANES):
    return jnp.pad(x, [(0, 0)] * (x.ndim - 1) + [(0, width - x.shape[-1])])


def _alibi_tables(seq_len):
    pos = jnp.arange(seq_len, dtype=F32)
    slopes = jnp.asarray([2.0 ** (-8.0 * (i + 1) / N_DIFF_HEADS) for i in range(N_DIFF_HEADS)], F32)
    u = jnp.stack(_split3(slopes[:, None] * LOG2E * pos[None, :]), axis=-1)
    one = jnp.ones_like(u)
    qb = _pad_lanes(jnp.concatenate([-u, one], axis=-1)).astype(BF16)
    kb = _pad_lanes(jnp.concatenate([one, u], axis=-1)).astype(BF16)
    return qb, kb


def kernel(x_prompt, x_sample, cache_fox_k, cache_fox_v, cache_fox_logf, cache_diff_k, cache_diff_v,
           page_table, w_in, b_forget, w_out, lambda_q1, lambda_k1, lambda_q2, lambda_k2, subln_g,
           ln1_g, ln1_b, ln2_g, ln2_b, w_router, b_router, w_gate, w_up, w_down):
    batch, seq_len, d_model = x_prompt.shape
    depth = w_in.shape[0]
    nseq = x_sample.shape[0]
    n_pool, page = cache_fox_k.shape[1], cache_fox_k.shape[2]
    n = batch * seq_len
    dn_alpha = (2 * depth) ** 0.25
    blk = min(512, seq_len)
    tm_proj = min(256, seq_len)
    tm_tok = min(512, n)
    tm_moe = min(1024, n)

    gate0 = 3 * FOX_WIDTH
    wm = jnp.concatenate([w_in[:, :, :gate0], w_in[:, :, gate0 + N_FOX_HEADS:]], axis=2).astype(BF16)
    wf = _pad_lanes(jnp.tile(w_in[:, :, gate0:gate0 + N_FOX_HEADS], (1, 1, 3))).astype(BF16)
    bfv = _pad_lanes(jnp.tile(b_forget, (1, 3)))[:, None, :]
    wo = w_out.astype(BF16)
    wr_hi = _round8(w_router)
    wr_lo = _round8(w_router - wr_hi)
    wr = _pad_lanes(jnp.concatenate([wr_hi, wr_lo], axis=1)).astype(BF16)
    br = _pad_lanes(b_router[None, :])
    wg = w_gate.astype(BF16)
    wu = w_up.astype(BF16)
    wd = w_down.astype(BF16)
    tri = jnp.tril(jnp.ones((tm_proj, tm_proj), BF16))
    dqb, dkb = _alibi_tables(seq_len)
    dqbT = jnp.transpose(dqb, (0, 2, 1))

    fkc = jnp.transpose(cache_fox_k, (0, 1, 3, 4, 2)).reshape(depth, n_pool, FOX_WIDTH, page)
    fvc = jnp.transpose(cache_fox_v, (0, 1, 3, 4, 2)).reshape(depth, n_pool, FOX_WIDTH, page)
    lfc = jnp.transpose(cache_fox_logf, (0, 1, 3, 2))
    dkc = cache_diff_k.reshape(depth, n_pool, page * N_DIFF_HEADS, 2 * HEAD_DIM)
    dvc = cache_diff_v.reshape(depth, n_pool, page * N_DIFF_HEADS, 2 * HEAD_DIM)
    group = math.gcd(8, page_table.shape[1])

    hp = x_prompt.reshape(n, d_model)
    hs = x_sample.reshape(nseq, d_model)
    p_new = None
    s_rows = [[] for _ in range(5)]
    for layer in range(depth):
        lambda_init = 0.8 - 0.6 * math.exp(-0.3 * layer)
        lam_args = (lambda_q1[layer][None], lambda_k1[layer][None], lambda_q2[layer][None],
                    lambda_k2[layer][None], subln_g[layer][None])
        ln1 = (ln1_g[layer][None], ln1_b[layer][None])
        ln2 = (ln2_g[layer][None], ln2_b[layer][None])

        (fqT, fkaug, fvT, cqbT, dqT, dkaug, dvT, *p_new) = _project_prompt(
            hp, wm[layer], wf[layer], bfv[layer], tri, dkb, p_new, batch=batch, seq_len=seq_len, tm=tm_proj)
        fox = _fox_attention(fqT, cqbT, fkaug, fvT, batch=batch, seq_len=seq_len, blk=blk)
        diff = _diff_attention(dqT, dqbT, dkaug, dvT, *lam_args[:4], subln_g[layer][:, None], batch=batch,
                               seq_len=seq_len, blk=blk, lambda_init=lambda_init)
        h1, h1b, comb = _out_proj(fox, diff, hp, wo[layer], *ln1, wr, br, tm=tm_tok, dn_alpha=dn_alpha)
        hp = _moe(h1b, h1, comb, wg[layer], wu[layer], wd[layer], *ln2, tm=tm_moe, dn_alpha=dn_alpha)

        ps, lfs = _project_sample(hs, wm[layer], wf[layer], bfv[layer])
        sfq, sfk, sfv, sdq, sdk, sdv = (ps[:, 512 * c:512 * (c + 1)] for c in range(6))
        lfs = lfs[:, :N_FOX_HEADS]
        sfox, sdiff = _decode_attention(page_table, layer, fkc, fvc, lfc, dkc, dvc, sfq, sfk, sfv, lfs * LOG2E,
                                        sdq, sdk, sdv, *lam_args, group=group, lambda_init=lambda_init)
        sfox = sfox.astype(BF16)
        sdiff = sdiff.astype(BF16)
        s1, s1b, scomb = _out_proj(sfox, sdiff, hs, wo[layer], *ln1, wr, br, tm=nseq, dn_alpha=dn_alpha)
        hs = _moe(s1b, s1, scomb, wg[layer], wu[layer], wd[layer], *ln2, tm=nseq, dn_alpha=dn_alpha)
        for lst, val in zip(s_rows, (
                sfk.reshape(nseq, 1, N_FOX_HEADS, HEAD_DIM), sfv.reshape(nseq, 1, N_FOX_HEADS, HEAD_DIM),
                lfs.reshape(nseq, 1, N_FOX_HEADS),
                sdk.reshape(nseq, 1, N_DIFF_HEADS, 2 * HEAD_DIM), sdv.reshape(nseq, 1, N_DIFF_HEADS, 2 * HEAD_DIM))):
            lst.append(val)

    fk_t, fv_t, lf_t, dk_s, dv_s = p_new
    heads_t = lambda a: jnp.transpose(a.reshape(depth, batch, N_FOX_HEADS, HEAD_DIM, seq_len), (0, 1, 4, 2, 3))
    outs_p = [heads_t(fk_t), heads_t(fv_t), jnp.transpose(lf_t, (0, 1, 3, 2)),
              dk_s.reshape(depth, batch, seq_len, N_DIFF_HEADS, 2 * HEAD_DIM),
              dv_s.reshape(depth, batch, seq_len, N_DIFF_HEADS, 2 * HEAD_DIM)]
    outs_s = [jnp.stack(r, 0) for r in s_rows]
    return (hp.reshape(batch, seq_len, d_model), hs.reshape(nseq, 1, d_model), *outs_p, *outs_s)
```

```python
import functools
import math

import jax
import jax.numpy as jnp
from jax import lax
from jax.experimental import pallas as pl
from jax.experimental.pallas import tpu as pltpu

F32 = jnp.float32
BF16 = jnp.bfloat16

HEAD_DIM = 64
N_FOX_HEADS = 8
N_DIFF_HEADS = 4
FOX_WIDTH = N_FOX_HEADS * HEAD_DIM
DIFF_WIDTH = N_DIFF_HEADS * 2 * HEAD_DIM
N_EXPERTS = 16
N_GROUPS = 4
EXPERTS_PER_GROUP = N_EXPERTS // N_GROUPS
LN_EPS = 1e-5
RMS_EPS = 1e-5
LOG2E = 1.4426950408889634
QK_SCALE = HEAD_DIM ** -0.5 * LOG2E
NEG = -1e30
LANES = 128
VMEM_LIMIT = 56 * 1024 * 1024
DIFF_V_ROWS = LANES + 16
MOE_CAP = 304


def _round8(x):
    t = x * 65537.0
    return t - (t - x)


def _split3(x):
    hi = _round8(x)
    r = x - hi
    mid = _round8(r)
    return hi, mid, r - mid


def _layer_norm(x, g, b):
    mu = jnp.mean(x, axis=-1, keepdims=True)
    xc = x - mu
    var = jnp.mean(xc * xc, axis=-1, keepdims=True)
    return xc * lax.rsqrt(var + LN_EPS) * g + b


def _nt_dot(a, b):
    return lax.dot_general(a, b, (((1,), (1,)), ((), ())), preferred_element_type=F32)


def _proj_kernel(*refs, blocks_per_seq, n_prev):
    x_ref, wm_ref, wf_ref, bf_ref, tri_ref, dkb_ref = refs[0:6]
    prev = refs[6:6 + 5] if n_prev else ()
    (fqT_ref, fkaug_ref, fvT_ref, cqbT_ref, dqT_ref, dkaug_ref, dvT_ref,
     fk_ref, fv_ref, lf_ref, dk_ref, dv_ref, carry_ref) = refs[6 + len(prev):]
    i = pl.program_id(0)
    for src, dst in zip(prev, (fk_ref, fv_ref, lf_ref, dk_ref, dv_ref)):
        for l in range(n_prev):
            dst[l] = src[l]

    @pl.when(i % blocks_per_seq == 0)
    def _():
        carry_ref[...] = jnp.zeros_like(carry_ref)

    xb = x_ref[...].astype(BF16)
    p = jnp.dot(xb, wm_ref[...], preferred_element_type=F32)
    tm = p.shape[0]
    lane = lax.broadcasted_iota(jnp.int32, (tm, LANES), 1)

    fqT_ref[...] = (p[:, 0:512] * QK_SCALE).T.astype(BF16)
    fk = p[:, 512:1024]
    fv = p[:, 1024:1536]
    dqT_ref[...] = (p[:, 1536:2048] * QK_SCALE).T.astype(BF16)
    dk = p[:, 2048:2560]
    dv = p[:, 2560:3072]
    fvT = fv.T
    dvT = dv.T
    fk_ref[n_prev] = fk.T
    fv_ref[n_prev] = fvT
    for h in range(N_DIFF_HEADS):
        hs = slice(LANES * h, LANES * (h + 1))
        dk_ref[n_prev, pl.ds(h, tm, stride=N_DIFF_HEADS), :] = dk[:, hs]
        dv_ref[n_prev, pl.ds(h, tm, stride=N_DIFF_HEADS), :] = dv[:, hs]

    z = jnp.dot(xb, wf_ref[...], preferred_element_type=F32) + bf_ref[...]
    lf = jnp.minimum(z, 0.0) - jnp.log1p(jnp.exp(-jnp.abs(z)))
    lf = jnp.where(lane < 24, lf, 0.0)
    lf_ref[n_prev] = lf.T[0:N_FOX_HEADS, :]

    hi, mid, lo = _split3(lf)
    packed = jnp.where(lane < 8, hi, jnp.where(lane < 16, mid, lo)).astype(BF16)
    zc = jnp.dot(tri_ref[...], packed, preferred_element_type=F32)
    c = zc + pltpu.roll(zc, 120, 1) + pltpu.roll(zc, 112, 1) + carry_ref[...]
    carry_ref[...] = c[tm - 1:tm, :]

    c2 = jnp.where(lane < 8, c * LOG2E, 0.0)
    c2rep = c2 + pltpu.roll(c2, 8, 1) + pltpu.roll(c2, 16, 1)
    h3, m3, l3 = _split3(c2rep)
    parts = jnp.where(lane < 8, h3, jnp.where(lane < 16, m3, l3))
    kparts = pltpu.roll(-parts, 32, 1)
    cqb = jnp.where((lane >= 32) & (lane < 56), 1.0, parts)
    kb = jnp.where(lane < 24, 1.0, kparts).astype(BF16)
    cqbT_ref[...] = cqb.T.astype(BF16)

    rowi = lax.broadcasted_iota(jnp.int32, (LANES, tm), 0)
    for j in range(4):
        sl = slice(LANES * j, LANES * (j + 1))
        fkaug_ref[j, :, 0:LANES] = fk[:, sl].astype(BF16)
        fkaug_ref[j, :, LANES:2 * LANES] = kb
        vpT = fvT[sl, :]
        fvT_ref[2 * j, 0] = jnp.where(rowi < HEAD_DIM, vpT, 1.0).astype(BF16)
        fvT_ref[2 * j + 1, 0] = jnp.where(rowi >= HEAD_DIM, vpT, 1.0).astype(BF16)
        dkaug_ref[j, :, 0:LANES] = dk[:, sl].astype(BF16)
        dkaug_ref[j, :, LANES:2 * LANES] = dkb_ref[j]
        dvT_ref[j, 0, 0:LANES, :] = dvT[sl, :].astype(BF16)
        dvT_ref[j, 0, LANES:DIFF_V_ROWS, :] = jnp.ones((DIFF_V_ROWS - LANES, tm), BF16)


def _project_prompt(x, wm, wf, bfv, tri, dkb, prev, *, batch, seq_len, tm):
    n, d = x.shape
    nb = n // tm
    bps = seq_len // tm
    n_prev = 0 if prev is None else prev[0].shape[0]
    nl = n_prev + 1
    row = lambda i: (i, 0)
    row3 = lambda i: (0, i, 0)
    const = lambda i: (0, 0)
    tmaj = lambda i: (0, i // bps, 0, i % bps)

    def stacked_specs(k):
        return [pl.BlockSpec((k, None, FOX_WIDTH, tm), tmaj), pl.BlockSpec((k, None, FOX_WIDTH, tm), tmaj),
                pl.BlockSpec((k, None, N_FOX_HEADS, tm), tmaj),
                pl.BlockSpec((k, tm * N_DIFF_HEADS, LANES), row3),
                pl.BlockSpec((k, tm * N_DIFF_HEADS, LANES), row3)]

    out_shape = (
        jax.ShapeDtypeStruct((512, n), BF16),
        jax.ShapeDtypeStruct((4, n, 256), BF16),
        jax.ShapeDtypeStruct((8, nb, LANES, tm), BF16),
        jax.ShapeDtypeStruct((LANES, n), BF16),
        jax.ShapeDtypeStruct((512, n), BF16),
        jax.ShapeDtypeStruct((4, n, 256), BF16),
        jax.ShapeDtypeStruct((4, nb, DIFF_V_ROWS, tm), BF16),
        jax.ShapeDtypeStruct((nl, batch, FOX_WIDTH, seq_len), F32),
        jax.ShapeDtypeStruct((nl, batch, FOX_WIDTH, seq_len), F32),
        jax.ShapeDtypeStruct((nl, batch, N_FOX_HEADS, seq_len), F32),
        jax.ShapeDtypeStruct((nl, n * N_DIFF_HEADS, LANES), F32),
        jax.ShapeDtypeStruct((nl, n * N_DIFF_HEADS, LANES), F32),
    )
    colb = lambda i: (0, i)
    chunk = lambda i: (0, i, 0, 0)
    out_specs = tuple([
        pl.BlockSpec((512, tm), colb),
        pl.BlockSpec((4, tm, 256), row3),
        pl.BlockSpec((8, 1, LANES, tm), chunk),
        pl.BlockSpec((LANES, tm), colb),
        pl.BlockSpec((512, tm), colb),
        pl.BlockSpec((4, tm, 256), row3),
        pl.BlockSpec((4, 1, DIFF_V_ROWS, tm), chunk),
    ] + stacked_specs(nl))
    in_specs = [
        pl.BlockSpec((tm, d), row),
        pl.BlockSpec(wm.shape, const),
        pl.BlockSpec(wf.shape, const),
        pl.BlockSpec(bfv.shape, const),
        pl.BlockSpec(tri.shape, const),
        pl.BlockSpec((4, tm, 128), lambda i: (0, i % bps, 0)),
    ] + (stacked_specs(n_prev) if n_prev else [])
    return pl.pallas_call(
        functools.partial(_proj_kernel, blocks_per_seq=bps, n_prev=n_prev),
        grid=(nb,), in_specs=in_specs, out_specs=out_specs, out_shape=out_shape,
        scratch_shapes=[pltpu.VMEM((1, LANES), F32)],
        compiler_params=pltpu.CompilerParams(dimension_semantics=("arbitrary",),
                                             vmem_limit_bytes=VMEM_LIMIT),
        name="proj_prompt",
    )(x, wm, wf, bfv, tri, dkb, *(prev or ()))


def _online_softmax_step(kb, qaug_ref, m_ref, acc_ref, vts_per_chain, masked):
    chains = range(len(vts_per_chain))
    nsub = len(vts_per_chain[0])
    sub = kb.shape[0] // nsub
    sts = [[jnp.dot(kb[k * sub:(k + 1) * sub], qaug_ref[c], preferred_element_type=F32) for c in chains]
           for k in range(nsub)]
    m_cur = [m_ref[c] for c in chains]
    for k in range(nsub):
        if masked:
            key = lax.broadcasted_iota(jnp.int32, sts[k][0].shape, 0) + k * sub
            qry = lax.broadcasted_iota(jnp.int32, sts[k][0].shape, 1)
            sts[k] = [jnp.where(key <= qry, st, NEG) for st in sts[k]]
        m_new = [jnp.maximum(m_cur[c], jnp.max(sts[k][c], axis=0, keepdims=True)) for c in chains]
        pts = [jnp.exp2(sts[k][c] - m_new[c]).astype(BF16) for c in chains]
        for c in chains:
            new = jnp.dot(vts_per_chain[c][k], pts[c], preferred_element_type=F32)
            acc_ref[c] = jnp.exp2(m_cur[c] - m_new[c]) * acc_ref[c] + new
        m_cur = m_new
    for c in chains:
        m_ref[c] = m_cur[c]


BLOCKS_PER_TRIP = 4


def _causal_sweep(qi, step):
    def body(p, carry):
        step(BLOCKS_PER_TRIP * p, BLOCKS_PER_TRIP, False)
        return carry
    lax.fori_loop(0, qi // BLOCKS_PER_TRIP, body, 0)

    done = (qi // BLOCKS_PER_TRIP) * BLOCKS_PER_TRIP
    size = BLOCKS_PER_TRIP // 2
    while size >= 1:
        here = done

        @pl.when((qi & size) != 0)
        def _():
            step(here, size, False)

        done = done + (qi & size)
        size //= 2
    step(qi, 1, True)


def _fox_attn_kernel(qT_ref, cqbT_ref, k_ref, vT_ref, o_ref, qaug_ref, acc_ref, m_ref, *, blk, nsub):
    pair = pl.program_id(1)
    qi = pl.program_id(2)
    rowi = lax.broadcasted_iota(jnp.int32, (LANES, blk), 0)
    qT = qT_ref[...].astype(F32)
    cT = cqbT_ref[...].astype(F32)
    bias_rows = (rowi < 24) | ((rowi >= 32) & (rowi < 56))
    for hh in range(2):
        h = 2 * pair + hh
        in_head = (rowi >= HEAD_DIM * hh) & (rowi < HEAD_DIM * (hh + 1))
        sel = ((rowi & 7) == h) & bias_rows
        qaug_ref[hh, 0:LANES, :] = jnp.where(in_head, qT, 0.0).astype(BF16)
        qaug_ref[hh, LANES:2 * LANES, :] = jnp.where(sel, cT, 0.0).astype(BF16)
    m_ref[...] = jnp.full(m_ref.shape, NEG, F32)
    acc_ref[...] = jnp.zeros(acc_ref.shape, F32)

    def step(ki, nblk, masked):
        start = pl.multiple_of(ki * blk, blk)
        kb = k_ref[pl.ds(start, nblk * blk), :]
        vts = [[vT_ref[hh, ki * nsub + c] for c in range(nblk * nsub)] for hh in range(2)]
        _online_softmax_step(kb, qaug_ref, m_ref, acc_ref, vts, masked)

    _causal_sweep(qi, step)
    a0 = acc_ref[0]
    a1 = acc_ref[1]
    o0 = a0 / a0[HEAD_DIM:HEAD_DIM + 1, :]
    o1 = a1 / a1[0:1, :]
    o_ref[...] = jnp.where(rowi < HEAD_DIM, o0, o1).T.astype(o_ref.dtype)


def _fox_attention(fqT, cqbT, fkaug, fvT, *, batch, seq_len, blk):
    n = fqT.shape[1]
    nq = seq_len // blk
    sub = fvT.shape[-1]
    return pl.pallas_call(
        functools.partial(_fox_attn_kernel, blk=blk, nsub=blk // sub),
        grid=(batch, 4, nq),
        in_specs=[
            pl.BlockSpec((LANES, blk), lambda b, p, i: (p, b * nq + i)),
            pl.BlockSpec((LANES, blk), lambda b, p, i: (0, b * nq + i)),
            pl.BlockSpec((None, seq_len, 256), lambda b, p, i: (p, b, 0)),
            pl.BlockSpec((2, seq_len // sub, LANES, sub), lambda b, p, i: (p, b, 0, 0)),
        ],
        out_specs=pl.BlockSpec((blk, LANES), lambda b, p, i: (b * nq + i, p)),
        out_shape=jax.ShapeDtypeStruct((n, FOX_WIDTH), BF16),
        scratch_shapes=[pltpu.VMEM((2, 256, blk), BF16), pltpu.VMEM((2, LANES, blk), F32),
                        pltpu.VMEM((2, 1, blk), F32)],
        compiler_params=pltpu.CompilerParams(
            dimension_semantics=("arbitrary", "arbitrary", "arbitrary"), vmem_limit_bytes=VMEM_LIMIT),
        name="fox_attn",
    )(fqT, cqbT, fkaug, fvT)


def _diff_lambda(lq1, lk1, lq2, lk2, lambda_init):
    return (jnp.exp(jnp.sum(lq1 * lk1, axis=1, keepdims=True))
            - jnp.exp(jnp.sum(lq2 * lk2, axis=1, keepdims=True)) + lambda_init)


def _diff_combine(o1, o2, lam, g, lambda_init, axis):
    a = o1 - lam * o2
    a = a * lax.rsqrt(jnp.mean(a * a, axis=axis, keepdims=True) + RMS_EPS)
    return a * g * (1.0 - lambda_init)


def _diff_attn_kernel(qT_ref, qbT_ref, k_ref, vT_ref, lq1_ref, lk1_ref, lq2_ref, lk2_ref, g_ref,
                      o_ref, qaug_ref, acc_ref, m_ref, *, blk, nsub, lambda_init):
    qi = pl.program_id(2)
    rowi = lax.broadcasted_iota(jnp.int32, (LANES, blk), 0)
    qT = qT_ref[...].astype(F32)
    for c in range(2):
        half = (rowi >= HEAD_DIM * c) & (rowi < HEAD_DIM * (c + 1))
        qaug_ref[c, 0:LANES, :] = jnp.where(half, qT, 0.0).astype(BF16)
        qaug_ref[c, LANES:2 * LANES, :] = qbT_ref[...]
    m_ref[...] = jnp.full(m_ref.shape, NEG, F32)
    acc_ref[...] = jnp.zeros(acc_ref.shape, F32)

    def step(ki, nblk, masked):
        start = pl.multiple_of(ki * blk, blk)
        kb = k_ref[pl.ds(start, nblk * blk), :]
        vts = [vT_ref[ki * nsub + c] for c in range(nblk * nsub)]
        _online_softmax_step(kb, qaug_ref, m_ref, acc_ref, [vts, vts], masked)

    _causal_sweep(qi, step)
    a1 = acc_ref[0]
    a2 = acc_ref[1]
    o1 = a1[0:LANES] / a1[LANES:LANES + 1]
    o2 = a2[0:LANES] / a2[LANES:LANES + 1]
    lam = _diff_lambda(lq1_ref[...], lk1_ref[...], lq2_ref[...], lk2_ref[...], lambda_init)
    o_ref[...] = _diff_combine(o1, o2, lam, g_ref[...], lambda_init, 0).T.astype(o_ref.dtype)


def _diff_attention(dqT, dqbT, dkaug, dvT, lq1, lk1, lq2, lk2, gcol, *, batch, seq_len, blk, lambda_init):
    n = dqT.shape[1]
    nq = seq_len // blk
    sub = dvT.shape[-1]
    vrows = dvT.shape[-2]
    vec = lambda b, h, i: (0, 0)
    return pl.pallas_call(
        functools.partial(_diff_attn_kernel, blk=blk, nsub=blk // sub, lambda_init=lambda_init),
        grid=(batch, N_DIFF_HEADS, nq),
        in_specs=[
            pl.BlockSpec((LANES, blk), lambda b, h, i: (h, b * nq + i)),
            pl.BlockSpec((None, LANES, blk), lambda b, h, i: (h, 0, i)),
            pl.BlockSpec((None, seq_len, 256), lambda b, h, i: (h, b, 0)),
            pl.BlockSpec((None, seq_len // sub, vrows, sub), lambda b, h, i: (h, b, 0, 0)),
            pl.BlockSpec((1, HEAD_DIM), vec), pl.BlockSpec((1, HEAD_DIM), vec),
            pl.BlockSpec((1, HEAD_DIM), vec), pl.BlockSpec((1, HEAD_DIM), vec),
            pl.BlockSpec((LANES, 1), vec),
        ],
        out_specs=pl.BlockSpec((blk, LANES), lambda b, h, i: (b * nq + i, h)),
        out_shape=jax.ShapeDtypeStruct((n, DIFF_WIDTH), BF16),
        scratch_shapes=[pltpu.VMEM((2, 256, blk), BF16), pltpu.VMEM((2, vrows, blk), F32),
                        pltpu.VMEM((2, 1, blk), F32)],
        compiler_params=pltpu.CompilerParams(
            dimension_semantics=("arbitrary", "arbitrary", "arbitrary"), vmem_limit_bytes=VMEM_LIMIT),
        name="diff_attn",
    )(dqT, dqbT, dkaug, dvT, lq1, lk1, lq2, lk2, gcol)


def _route(logits, lane):
    lanef = lane.astype(F32)
    lg = jnp.where(lane < N_EXPERTS, logits, NEG)
    mx = jnp.max(lg, axis=1, keepdims=True)
    e = jnp.exp(lg - mx)
    scores = e / jnp.sum(e, axis=1, keepdims=True)
    best = None
    for g in range(N_GROUPS):
        ing = (lane >= EXPERTS_PER_GROUP * g) & (lane < EXPERTS_PER_GROUP * (g + 1))
        sg = jnp.where(ing, scores, -1.0)
        m1 = jnp.max(sg, axis=1, keepdims=True)
        i1 = jnp.min(jnp.where(sg == m1, lanef, 999.0), axis=1, keepdims=True)
        sg2 = jnp.where(lanef == i1, -1.0, sg)
        m2 = jnp.max(sg2, axis=1, keepdims=True)
        i2 = jnp.min(jnp.where(sg2 == m2, lanef, 999.0), axis=1, keepdims=True)
        cand = (m1 + m2, m1, m2, i1, i2)
        if best is None:
            best = cand
        else:
            upd = cand[0] > best[0]
            best = tuple(jnp.where(upd, cn, bs) for cn, bs in zip(cand, best))
    _, v1, v2, i1, i2 = best
    den = v1 + v2
    return jnp.where(lanef == i1, v1 / den, 0.0) + jnp.where(lanef == i2, v2 / den, 0.0)


def _outproj_kernel(fox_ref, diff_ref, h_ref, wo_ref, g_ref, b_ref, wr_ref, br_ref,
                    h1_ref, h1b_ref, comb_ref, *, dn_alpha):
    mixed = jnp.concatenate([fox_ref[...], diff_ref[...]], axis=1)
    a = jnp.dot(mixed, wo_ref[...], preferred_element_type=F32)
    h1 = _layer_norm(dn_alpha * h_ref[...] + a, g_ref[...], b_ref[...])
    h1_ref[...] = h1
    x_hi = h1.astype(BF16)
    h1b_ref[...] = x_hi
    x_lo = (h1 - x_hi.astype(F32)).astype(BF16)
    r1 = jnp.dot(x_hi, wr_ref[...], preferred_element_type=F32)
    r2 = jnp.dot(x_lo, wr_ref[...], preferred_element_type=F32)
    logits = r1 + pltpu.roll(r1, 112, 1) + r2 + br_ref[...]
    lane = lax.broadcasted_iota(jnp.int32, logits.shape, 1)
    comb_ref[...] = _route(logits, lane)


def _out_proj(fox, diff, h, wo, g, b, wr, br, *, tm, dn_alpha):
    n, d = h.shape
    row = lambda i: (i, 0)
    const = lambda i: (0, 0)
    return pl.pallas_call(
        functools.partial(_outproj_kernel, dn_alpha=dn_alpha),
        grid=(n // tm,),
        in_specs=[
            pl.BlockSpec((tm, 512), row), pl.BlockSpec((tm, 512), row), pl.BlockSpec((tm, d), row),
            pl.BlockSpec(wo.shape, const), pl.BlockSpec((1, d), const), pl.BlockSpec((1, d), const),
            pl.BlockSpec(wr.shape, const), pl.BlockSpec((1, LANES), const),
        ],
        out_specs=(pl.BlockSpec((tm, d), row), pl.BlockSpec((tm, d), row), pl.BlockSpec((tm, LANES), row)),
        out_shape=(jax.ShapeDtypeStruct((n, d), F32), jax.ShapeDtypeStruct((n, d), BF16),
                   jax.ShapeDtypeStruct((n, LANES), F32)),
        compiler_params=pltpu.CompilerParams(dimension_semantics=("arbitrary",),
                                             vmem_limit_bytes=VMEM_LIMIT),
        name="out_proj_ln_router",
    )(fox, diff, h, wo, g, b, wr, br)


def _moe_kernel(x_ref, h1_ref, comb_ref, wg_ref, wu_ref, wd_ref, g_ref, b_ref, o_ref, acc_ref,
                *, dn_alpha):
    e = pl.program_id(1)

    @pl.when(e == 0)
    def _():
        acc_ref[...] = jnp.zeros_like(acc_ref)

    x = x_ref[...]
    hg = jnp.dot(x, wg_ref[...], preferred_element_type=F32)
    hu = jnp.dot(x, wu_ref[...], preferred_element_type=F32)
    comb = comb_ref[...]
    lane = lax.broadcasted_iota(jnp.int32, comb.shape, 1)
    ce = jnp.sum(jnp.where(lane == e, comb, 0.0), axis=1, keepdims=True)
    act = hg * jax.nn.sigmoid(hg) * hu * ce
    acc_ref[...] += jnp.dot(act.astype(BF16), wd_ref[...], preferred_element_type=F32)

    @pl.when(e == N_EXPERTS - 1)
    def _():
        o_ref[...] = _layer_norm(dn_alpha * h1_ref[...] + acc_ref[...], g_ref[...], b_ref[...])


def _moe(xb, h1, comb, wg, wu, wd, g, b, *, tm, dn_alpha):
    n, d = h1.shape
    f = wg.shape[-1]
    row = lambda i, e: (i, 0)
    const = lambda i, e: (0, 0)
    return pl.pallas_call(
        functools.partial(_moe_kernel, dn_alpha=dn_alpha),
        grid=(n // tm, N_EXPERTS),
        in_specs=[
            pl.BlockSpec((tm, d), row), pl.BlockSpec((tm, d), row), pl.BlockSpec((tm, LANES), row),
            pl.BlockSpec((None, d, f), lambda i, e: (e, 0, 0)),
            pl.BlockSpec((None, d, f), lambda i, e: (e, 0, 0)),
            pl.BlockSpec((None, f, d), lambda i, e: (e, 0, 0)),
            pl.BlockSpec((1, d), const), pl.BlockSpec((1, d), const),
        ],
        out_specs=pl.BlockSpec((tm, d), row),
        out_shape=jax.ShapeDtypeStruct((n, d), F32),
        scratch_shapes=[pltpu.VMEM((tm, d), F32)],
        compiler_params=pltpu.CompilerParams(dimension_semantics=("arbitrary", "arbitrary"),
                                             vmem_limit_bytes=VMEM_LIMIT),
        name="moe_ln",
    )(xb, h1, comb, wg, wu, wd, g, b)


def _t_dot(a, b):
    return lax.dot_general(a, b, (((0,), (0,)), ((), ())), preferred_element_type=F32)


def _moe_grouped_kernel(x_ref, h1_ref, comb_ref, wg_ref, wu_ref, wd_ref, g_ref, b_ref, tri_ref, o_ref,
                        acc_ref, sel_ref, xg_ref, gates_ref, yg_ref, memb_ref, rank_ref, cnt_ref,
                        *, dn_alpha, cap):
    e = pl.program_id(1)
    grp = e // EXPERTS_PER_GROUP
    tm = x_ref.shape[0]
    lane = lax.broadcasted_iota(jnp.int32, (cap, LANES), 1)

    @pl.when(e == 0)
    def _():
        acc_ref[...] = jnp.zeros_like(acc_ref)
        comb_t = comb_ref[...].T
        rows = []
        for gg in range(N_GROUPS):
            w = jnp.sum(comb_t[EXPERTS_PER_GROUP * gg:EXPERTS_PER_GROUP * (gg + 1)], axis=0, keepdims=True)
            rows.append(jnp.where(w > 0.0, 1.0, 0.0))
        memb = jnp.concatenate(rows + [jnp.zeros((8 - N_GROUPS, tm), F32)], axis=0)
        memb_ref[...] = memb
        rank_ref[...] = jnp.dot(memb.astype(BF16), tri_ref[...], preferred_element_type=F32)
        for gg in range(N_GROUPS):
            cnt_ref[gg] = jnp.sum(rows[gg]).astype(jnp.int32)

    def select(chunk):
        slot = lax.broadcasted_iota(jnp.int32, (cap, tm), 0).astype(F32) + (chunk * cap).astype(F32)
        hit = (rank_ref[pl.ds(grp, 1), :] == slot) & (memb_ref[pl.ds(grp, 1), :] > 0.0)
        return jnp.where(hit, 1.0, 0.0).astype(BF16)

    def gather(sel):
        xg = jnp.dot(sel, x_ref[...], preferred_element_type=F32).astype(BF16)
        comb = comb_ref[...]
        c_hi = _round8(comb)
        c_lo = _round8(comb - c_hi)
        gates = (jnp.dot(sel, c_hi.astype(BF16), preferred_element_type=F32)
                 + jnp.dot(sel, c_lo.astype(BF16), preferred_element_type=F32))
        return xg, gates

    def expert(xg, gates):
        hg = jnp.dot(xg, wg_ref[...], preferred_element_type=F32)
        hu = jnp.dot(xg, wu_ref[...], preferred_element_type=F32)
        ge = jnp.sum(jnp.where(lane == e, gates, 0.0), axis=1, keepdims=True)
        act = hg * jax.nn.sigmoid(hg) * hu * ge
        return jnp.dot(act.astype(BF16), wd_ref[...], preferred_element_type=F32)

    def scatter(sel, y):
        y_hi = y.astype(BF16)
        y_lo = (y - y_hi.astype(F32)).astype(BF16)
        acc_ref[...] += _t_dot(sel, y_hi) + _t_dot(sel, y_lo)

    @pl.when(e % EXPERTS_PER_GROUP == 0)
    def _():
        sel = select(jnp.int32(0))
        sel_ref[...] = sel
        xg, gates = gather(sel)
        xg_ref[...] = xg
        gates_ref[...] = gates
        yg_ref[...] = jnp.zeros_like(yg_ref)

    yg_ref[...] += expert(xg_ref[...], gates_ref[...])

    def extra_chunk(chunk, carry):
        sel = select(chunk)
        xg, gates = gather(sel)
        scatter(sel, expert(xg, gates))
        return carry

    lax.fori_loop(1, (cnt_ref[grp] + cap - 1) // cap, extra_chunk, 0)

    @pl.when(e % EXPERTS_PER_GROUP == EXPERTS_PER_GROUP - 1)
    def _():
        scatter(sel_ref[...], yg_ref[...])

    @pl.when(e == N_EXPERTS - 1)
    def _():
        o_ref[...] = _layer_norm(dn_alpha * h1_ref[...] + acc_ref[...], g_ref[...], b_ref[...])


def _moe_grouped(xb, h1, comb, wg, wu, wd, g, b, *, tm, cap, dn_alpha):
    n, d = h1.shape
    f = wg.shape[-1]
    row = lambda i, e: (i, 0)
    const = lambda i, e: (0, 0)
    t = jnp.arange(tm)
    tri = (t[:, None] < t[None, :]).astype(BF16)
    return pl.pallas_call(
        functools.partial(_moe_grouped_kernel, dn_alpha=dn_alpha, cap=cap),
        grid=(n // tm, N_EXPERTS),
        in_specs=[
            pl.BlockSpec((tm, d), row), pl.BlockSpec((tm, d), row), pl.BlockSpec((tm, LANES), row),
            pl.BlockSpec((None, d, f), lambda i, e: (e, 0, 0)),
            pl.BlockSpec((None, d, f), lambda i, e: (e, 0, 0)),
            pl.BlockSpec((None, f, d), lambda i, e: (e, 0, 0)),
            pl.BlockSpec((1, d), const), pl.BlockSpec((1, d), const),
            pl.BlockSpec((tm, tm), const),
        ],
        out_specs=pl.BlockSpec((tm, d), row),
        out_shape=jax.ShapeDtypeStruct((n, d), F32),
        scratch_shapes=[pltpu.VMEM((tm, d), F32), pltpu.VMEM((cap, tm), BF16), pltpu.VMEM((cap, d), BF16),
                        pltpu.VMEM((cap, LANES), F32), pltpu.VMEM((cap, d), F32),
                        pltpu.VMEM((8, tm), F32), pltpu.VMEM((8, tm), F32), pltpu.SMEM((N_GROUPS,), jnp.int32)],
        compiler_params=pltpu.CompilerParams(dimension_semantics=("arbitrary", "arbitrary"),
                                             vmem_limit_bytes=VMEM_LIMIT),
        name="moe_grouped_ln",
    )(xb, h1, comb, wg, wu, wd, g, b, tri)


def _proj_sample_kernel(x_ref, wm_ref, wf_ref, bf_ref, p_ref, lf_ref):
    xb = x_ref[...].astype(BF16)
    p = jnp.dot(xb, wm_ref[...], preferred_element_type=F32)
    col = lax.broadcasted_iota(jnp.int32, p.shape, 1)
    is_q = (col < 512) | ((col >= 1536) & (col < 2048))
    p_ref[...] = jnp.where(is_q, p * QK_SCALE, p)
    z = jnp.dot(xb, wf_ref[...], preferred_element_type=F32) + bf_ref[...]
    lf_ref[...] = jnp.minimum(z, 0.0) - jnp.log1p(jnp.exp(-jnp.abs(z)))


def _project_sample(x, wm, wf, bfv):
    n = x.shape[0]
    return pl.pallas_call(
        _proj_sample_kernel,
        out_shape=(jax.ShapeDtypeStruct((n, wm.shape[1]), F32), jax.ShapeDtypeStruct((n, LANES), F32)),
        compiler_params=pltpu.CompilerParams(vmem_limit_bytes=VMEM_LIMIT),
        name="proj_sample",
    )(x, wm, wf, bfv)


def _rows_per_head(x, rows):
    return jnp.concatenate([jnp.broadcast_to(x[h:h + 1, :], (rows, x.shape[1])) for h in range(x.shape[0])],
                           axis=0)


def _decode_attn_kernel(pt_ref, *refs, group, page, n_pages, lambda_init):
    g = group
    fkc = refs[0:g]
    fvc = refs[g:2 * g]
    lfc = refs[2 * g:3 * g]
    dkc = refs[3 * g:4 * g]
    dvc = refs[4 * g:5 * g]
    (fqc_ref, fq_ref, fkn_ref, fvnc_ref, lfn_ref, tri_ref, dq_ref, dkn_ref, dvn_ref,
     lq1_ref, lk1_ref, lq2_ref, lk2_ref, g_ref,
     fo_ref, do_ref,
     qb_ref, mf_ref, lf_ref, accf_ref, carry_ref, md_ref, ld_ref, accd_ref) = refs[5 * g:]
    j = pl.program_id(1)
    hd = N_FOX_HEADS * HEAD_DIM

    @pl.when(j == 0)
    def _():
        qb_ref[...] = jnp.broadcast_to(fqc_ref[...], (hd, LANES))
        carry_ref[...] = lfn_ref[...]
        mf_ref[...] = jnp.full(mf_ref.shape, NEG, F32)
        md_ref[...] = jnp.full(md_ref.shape, NEG, F32)
        lf_ref[...] = jnp.zeros_like(lf_ref)
        ld_ref[...] = jnp.zeros_like(ld_ref)
        accf_ref[...] = jnp.zeros_like(accf_ref)
        accd_ref[...] = jnp.zeros_like(accd_ref)

    qb = qb_ref[...]
    tri = tri_ref[...]
    carry = carry_ref[...]
    s_list = []
    for i in range(g):
        s = jnp.sum((fkc[i][...] * qb).reshape(N_FOX_HEADS, HEAD_DIM, LANES), axis=1)
        lfp = lfc[i][...] * LOG2E
        within = sum(jnp.dot(t.astype(BF16), tri, preferred_element_type=F32) for t in _split3(lfp))
        s_list.append(s + within + carry)
        carry = carry + jnp.sum(lfp, axis=1, keepdims=True)
    carry_ref[...] = carry
    m_prev = mf_ref[...]
    m_new = m_prev
    for s in s_list:
        m_new = jnp.maximum(m_new, jnp.max(s, axis=1, keepdims=True))
    alpha = jnp.exp2(m_prev - m_new)
    l_new = alpha * lf_ref[...]
    acc = _rows_per_head(jnp.broadcast_to(alpha, (N_FOX_HEADS, LANES)), HEAD_DIM) * accf_ref[...]
    for i in range(g):
        p = jnp.exp2(s_list[i] - m_new)
        l_new = l_new + jnp.sum(p, axis=1, keepdims=True)
        acc = acc + _rows_per_head(p, HEAD_DIM) * fvc[i][...]
    accf_ref[...] = acc
    lf_ref[...] = l_new
    mf_ref[...] = m_new

    dq4 = dq_ref[...]
    lane4 = lax.broadcasted_iota(jnp.int32, dq4.shape, 1)
    dq8 = jnp.concatenate([jnp.where(lane4 < HEAD_DIM, dq4, 0.0), jnp.where(lane4 >= HEAD_DIM, dq4, 0.0)], axis=0)
    nd = page * N_DIFF_HEADS
    rowd = lax.broadcasted_iota(jnp.int32, (8, nd), 0)
    cold = lax.broadcasted_iota(jnp.int32, (8, nd), 1)
    hrow = rowd & (N_DIFF_HEADS - 1)
    slope2 = jnp.exp2(-(8.0 / N_DIFF_HEADS) * (hrow.astype(F32) + 1.0)) * LOG2E
    keep = (cold & (N_DIFF_HEADS - 1)) == hrow
    t_in_page = cold // N_DIFF_HEADS
    sd_list = []
    for i in range(g):
        first_pos = (n_pages - 1 - (j * g + i)) * page
        dist = (n_pages * page - first_pos - t_in_page).astype(F32)
        sd = _nt_dot(dq8, dkc[i][...])
        sd_list.append(jnp.where(keep, sd - slope2 * dist, NEG))
    m_prev = md_ref[...]
    m_new = m_prev
    for sd in sd_list:
        m_new = jnp.maximum(m_new, jnp.max(sd, axis=1, keepdims=True))
    alpha = jnp.exp2(m_prev - m_new)
    l_new = alpha * ld_ref[...]
    accd = alpha * accd_ref[...]
    for i in range(g):
        p = jnp.exp2(sd_list[i] - m_new)
        l_new = l_new + jnp.sum(p, axis=1, keepdims=True)
        accd = accd + jnp.dot(p, dvc[i][...], preferred_element_type=F32)
    accd_ref[...] = accd
    ld_ref[...] = l_new
    md_ref[...] = m_new

    @pl.when(j == pl.num_programs(1) - 1)
    def _():
        fq = fq_ref[...]
        s_self = jnp.sum(fq * fkn_ref[...], axis=1, keepdims=True)
        m_prev = mf_ref[...]
        m_fin = jnp.maximum(m_prev, s_self)
        a = jnp.exp2(m_prev - m_fin)
        p_self = jnp.exp2(s_self - m_fin)
        l_fin = a * lf_ref[...] + p_self
        num = jnp.sum(accf_ref[...], axis=1, keepdims=True)
        fo_ref[...] = ((_rows_per_head(a, HEAD_DIM) * num + _rows_per_head(p_self, HEAD_DIM) * fvnc_ref[...])
                       / _rows_per_head(l_fin, HEAD_DIM))

        dk8 = jnp.concatenate([dkn_ref[...], dkn_ref[...]], axis=0)
        dv8 = jnp.concatenate([dvn_ref[...], dvn_ref[...]], axis=0)
        s_self = jnp.sum(dq8 * dk8, axis=1, keepdims=True)
        m_prev = md_ref[...]
        m_fin = jnp.maximum(m_prev, s_self)
        a = jnp.exp2(m_prev - m_fin)
        p_self = jnp.exp2(s_self - m_fin)
        od = (a * accd_ref[...] + p_self * dv8) / (a * ld_ref[...] + p_self)
        lam = _diff_lambda(lq1_ref[...], lk1_ref[...], lq2_ref[...], lk2_ref[...], lambda_init)
        do_ref[...] = _diff_combine(od[0:N_DIFF_HEADS], od[N_DIFF_HEADS:], lam, g_ref[...], lambda_init, -1)


def _decode_attention(page_table, layer, fkc, fvc, lfc, dkc, dvc, fq, fkn, fvn, lfn, dq, dkn, dvn,
                      lq1, lk1, lq2, lk2, gain, *, group, lambda_init):
    nseq, n_pages = page_table.shape
    page = fkc.shape[-1]
    hd = N_FOX_HEADS * HEAD_DIM
    seq3 = lambda b, j, pt: (b, 0, 0)
    vec = lambda b, j, pt: (0, 0)

    def pool(i):
        return lambda b, j, pt: (layer, pt[b, n_pages - 1 - (j * group + i)], 0, 0)

    def paged(arr):
        return [pl.BlockSpec((None, None) + arr.shape[2:], pool(i)) for i in range(group)]

    tri = (jnp.arange(page)[:, None] > jnp.arange(page)[None, :]).astype(BF16)
    f3 = lambda a: a.reshape(nseq, N_FOX_HEADS, HEAD_DIM)
    d3 = lambda a: a.reshape(nseq, N_DIFF_HEADS, 2 * HEAD_DIM)
    col = lambda a: a.reshape(nseq, -1, 1)
    in_specs = (paged(fkc) + paged(fvc) + paged(lfc) + paged(dkc) + paged(dvc) + [
        pl.BlockSpec((None, hd, 1), seq3),
        pl.BlockSpec((None, N_FOX_HEADS, HEAD_DIM), seq3),
        pl.BlockSpec((None, N_FOX_HEADS, HEAD_DIM), seq3),
        pl.BlockSpec((None, hd, 1), seq3),
        pl.BlockSpec((None, N_FOX_HEADS, 1), seq3),
        pl.BlockSpec((page, page), vec),
        pl.BlockSpec((None, N_DIFF_HEADS, LANES), seq3),
        pl.BlockSpec((None, N_DIFF_HEADS, LANES), seq3),
        pl.BlockSpec((None, N_DIFF_HEADS, LANES), seq3),
        pl.BlockSpec((1, HEAD_DIM), vec), pl.BlockSpec((1, HEAD_DIM), vec),
        pl.BlockSpec((1, HEAD_DIM), vec), pl.BlockSpec((1, HEAD_DIM), vec),
        pl.BlockSpec((1, LANES), vec),
    ])
    args = ([fkc] * group + [fvc] * group + [lfc] * group + [dkc] * group + [dvc] * group
            + [col(fq), f3(fq), f3(fkn), col(fvn), col(lfn), tri, d3(dq), d3(dkn), d3(dvn),
               lq1, lk1, lq2, lk2, gain])
    fo, do = pl.pallas_call(
        functools.partial(_decode_attn_kernel, group=group, page=page, n_pages=n_pages, lambda_init=lambda_init),
        grid_spec=pltpu.PrefetchScalarGridSpec(
            num_scalar_prefetch=1, grid=(nseq, n_pages // group), in_specs=in_specs,
            out_specs=(pl.BlockSpec((None, hd, 1), seq3),
                       pl.BlockSpec((None, N_DIFF_HEADS, LANES), seq3)),
            scratch_shapes=[pltpu.VMEM((hd, LANES), F32),
                            pltpu.VMEM((8, 1), F32), pltpu.VMEM((8, 1), F32), pltpu.VMEM((hd, LANES), F32),
                            pltpu.VMEM((8, 1), F32),
                            pltpu.VMEM((8, 1), F32), pltpu.VMEM((8, 1), F32), pltpu.VMEM((8, LANES), F32)]),
        out_shape=(jax.ShapeDtypeStruct((nseq, hd, 1), F32),
                   jax.ShapeDtypeStruct((nseq, N_DIFF_HEADS, LANES), F32)),
        compiler_params=pltpu.CompilerParams(dimension_semantics=("arbitrary", "arbitrary"),
                                             vmem_limit_bytes=VMEM_LIMIT),
        name="decode_attn",
    )(page_table, *args)
    return fo.reshape(nseq, hd), do.reshape(nseq, DIFF_WIDTH)


def _pad_lanes(x, width=LANES):
    return jnp.pad(x, [(0, 0)] * (x.ndim - 1) + [(0, width - x.shape[-1])])


def _alibi_tables(seq_len):
    pos = jnp.arange(seq_len, dtype=F32)
    slopes = jnp.asarray([2.0 ** (-8.0 * (i + 1) / N_DIFF_HEADS) for i in range(N_DIFF_HEADS)], F32)
    u = jnp.stack(_split3(slopes[:, None] * LOG2E * pos[None, :]), axis=-1)
    one = jnp.ones_like(u)
    qb = _pad_lanes(jnp.concatenate([-u, one], axis=-1)).astype(BF16)
    kb = _pad_lanes(jnp.concatenate([one, u], axis=-1)).astype(BF16)
    return qb, kb


def kernel(x_prompt, x_sample, cache_fox_k, cache_fox_v, cache_fox_logf, cache_diff_k, cache_diff_v,
           page_table, w_in, b_forget, w_out, lambda_q1, lambda_k1, lambda_q2, lambda_k2, subln_g,
           ln1_g, ln1_b, ln2_g, ln2_b, w_router, b_router, w_gate, w_up, w_down):
    batch, seq_len, d_model = x_prompt.shape
    depth = w_in.shape[0]
    nseq = x_sample.shape[0]
    n_pool, page = cache_fox_k.shape[1], cache_fox_k.shape[2]
    n = batch * seq_len
    dn_alpha = (2 * depth) ** 0.25
    blk = min(512, seq_len)
    tm_proj = min(256, seq_len)
    tm_tok = min(512, n)
    tm_moe = min(1024, n)

    gate0 = 3 * FOX_WIDTH
    wm = jnp.concatenate([w_in[:, :, :gate0], w_in[:, :, gate0 + N_FOX_HEADS:]], axis=2).astype(BF16)
    wf = _pad_lanes(jnp.tile(w_in[:, :, gate0:gate0 + N_FOX_HEADS], (1, 1, 3))).astype(BF16)
    bfv = _pad_lanes(jnp.tile(b_forget, (1, 3)))[:, None, :]
    wo = w_out.astype(BF16)
    wr_hi = _round8(w_router)
    wr_lo = _round8(w_router - wr_hi)
    wr = _pad_lanes(jnp.concatenate([wr_hi, wr_lo], axis=1)).astype(BF16)
    br = _pad_lanes(b_router[None, :])
    wg = w_gate.astype(BF16)
    wu = w_up.astype(BF16)
    wd = w_down.astype(BF16)
    tri = jnp.tril(jnp.ones((tm_proj, tm_proj), BF16))
    dqb, dkb = _alibi_tables(seq_len)
    dqbT = jnp.transpose(dqb, (0, 2, 1))

    fkc = jnp.transpose(cache_fox_k, (0, 1, 3, 4, 2)).reshape(depth, n_pool, FOX_WIDTH, page)
    fvc = jnp.transpose(cache_fox_v, (0, 1, 3, 4, 2)).reshape(depth, n_pool, FOX_WIDTH, page)
    lfc = jnp.transpose(cache_fox_logf, (0, 1, 3, 2))
    dkc = cache_diff_k.reshape(depth, n_pool, page * N_DIFF_HEADS, 2 * HEAD_DIM)
    dvc = cache_diff_v.reshape(depth, n_pool, page * N_DIFF_HEADS, 2 * HEAD_DIM)
    group = math.gcd(8, page_table.shape[1])

    hp = x_prompt.reshape(n, d_model)
    hs = x_sample.reshape(nseq, d_model)
    p_new = None
    s_rows = [[] for _ in range(5)]
    for layer in range(depth):
        lambda_init = 0.8 - 0.6 * math.exp(-0.3 * layer)
        lam_args = (lambda_q1[layer][None], lambda_k1[layer][None], lambda_q2[layer][None],
                    lambda_k2[layer][None], subln_g[layer][None])
        ln1 = (ln1_g[layer][None], ln1_b[layer][None])
        ln2 = (ln2_g[layer][None], ln2_b[layer][None])

        (fqT, fkaug, fvT, cqbT, dqT, dkaug, dvT, *p_new) = _project_prompt(
            hp, wm[layer], wf[layer], bfv[layer], tri, dkb, p_new, batch=batch, seq_len=seq_len, tm=tm_proj)
        fox = _fox_attention(fqT, cqbT, fkaug, fvT, batch=batch, seq_len=seq_len, blk=blk)
        diff = _diff_attention(dqT, dqbT, dkaug, dvT, *lam_args[:4], subln_g[layer][:, None], batch=batch,
                               seq_len=seq_len, blk=blk, lambda_init=lambda_init)
        h1, h1b, comb = _out_proj(fox, diff, hp, wo[layer], *ln1, wr, br, tm=tm_tok, dn_alpha=dn_alpha)
        hp = _moe_grouped(h1b, h1, comb, wg[layer], wu[layer], wd[layer], *ln2, tm=tm_moe,
                          cap=min(MOE_CAP, tm_moe), dn_alpha=dn_alpha)

        ps, lfs = _project_sample(hs, wm[layer], wf[layer], bfv[layer])
        sfq, sfk, sfv, sdq, sdk, sdv = (ps[:, 512 * c:512 * (c + 1)] for c in range(6))
        lfs = lfs[:, :N_FOX_HEADS]
        sfox, sdiff = _decode_attention(page_table, layer, fkc, fvc, lfc, dkc, dvc, sfq, sfk, sfv, lfs * LOG2E,
                                        sdq, sdk, sdv, *lam_args, group=group, lambda_init=lambda_init)
        sfox = sfox.astype(BF16)
        sdiff = sdiff.astype(BF16)
        s1, s1b, scomb = _out_proj(sfox, sdiff, hs, wo[layer], *ln1, wr, br, tm=nseq, dn_alpha=dn_alpha)
        hs = _moe(s1b, s1, scomb, wg[layer], wu[layer], wd[layer], *ln2, tm=nseq, dn_alpha=dn_alpha)
        for lst, val in zip(s_rows, (
                sfk.reshape(nseq, 1, N_FOX_HEADS, HEAD_DIM), sfv.reshape(nseq, 1, N_FOX_HEADS, HEAD_DIM),
                lfs.reshape(nseq, 1, N_FOX_HEADS),
                sdk.reshape(nseq, 1, N_DIFF_HEADS, 2 * HEAD_DIM), sdv.reshape(nseq, 1, N_DIFF_HEADS, 2 * HEAD_DIM))):
            lst.append(val)

    fk_t, fv_t, lf_t, dk_s, dv_s = p_new
    heads_t = lambda a: jnp.transpose(a.reshape(depth, batch, N_FOX_HEADS, HEAD_DIM, seq_len), (0, 1, 4, 2, 3))
    outs_p = [heads_t(fk_t), heads_t(fv_t), jnp.transpose(lf_t, (0, 1, 3, 2)),
              dk_s.reshape(depth, batch, seq_len, N_DIFF_HEADS, 2 * HEAD_DIM),
              dv_s.reshape(depth, batch, seq_len, N_DIFF_HEADS, 2 * HEAD_DIM)]
    outs_s = [jnp.stack(r, 0) for r in s_rows]
    return (hp.reshape(batch, seq_len, d_model), hs.reshape(nseq, 1, d_model), *outs_p, *outs_s)
```

```python
import functools
import math

import jax
import jax.numpy as jnp
from jax import lax
from jax.experimental import pallas as pl
from jax.experimental.pallas import tpu as pltpu

F32 = jnp.float32
BF16 = jnp.bfloat16

HEAD_DIM = 64
N_FOX_HEADS = 8
N_DIFF_HEADS = 4
FOX_WIDTH = N_FOX_HEADS * HEAD_DIM
DIFF_WIDTH = N_DIFF_HEADS * 2 * HEAD_DIM
N_EXPERTS = 16
N_GROUPS = 4
EXPERTS_PER_GROUP = N_EXPERTS // N_GROUPS
LN_EPS = 1e-5
RMS_EPS = 1e-5
LOG2E = 1.4426950408889634
QK_SCALE = HEAD_DIM ** -0.5 * LOG2E
NEG = -1e30
LANES = 128
VMEM_LIMIT = 56 * 1024 * 1024
DIFF_V_ROWS = LANES + 16
MOE_CAP = 304


def _round8(x):
    t = x * 65537.0
    return t - (t - x)


def _split3(x):
    hi = _round8(x)
    r = x - hi
    mid = _round8(r)
    return hi, mid, r - mid


def _layer_norm(x, g, b):
    mu = jnp.mean(x, axis=-1, keepdims=True)
    xc = x - mu
    var = jnp.mean(xc * xc, axis=-1, keepdims=True)
    return xc * lax.rsqrt(var + LN_EPS) * g + b


def _nt_dot(a, b):
    return lax.dot_general(a, b, (((1,), (1,)), ((), ())), preferred_element_type=F32)


def _proj_kernel(*refs, blocks_per_seq, n_prev):
    x_ref, wm_ref, wf_ref, bf_ref, tri_ref, dkb_ref = refs[0:6]
    prev = refs[6:6 + 5] if n_prev else ()
    (fqT_ref, fkaug_ref, fvT_ref, cqbT_ref, dqT_ref, dkaug_ref, dvT_ref,
     fk_ref, fv_ref, lf_ref, dk_ref, dv_ref, carry_ref) = refs[6 + len(prev):]
    i = pl.program_id(0)
    for src, dst in zip(prev, (fk_ref, fv_ref, lf_ref, dk_ref, dv_ref)):
        for l in range(n_prev):
            dst[l] = src[l]

    @pl.when(i % blocks_per_seq == 0)
    def _():
        carry_ref[...] = jnp.zeros_like(carry_ref)

    xb = x_ref[...].astype(BF16)
    p = jnp.dot(xb, wm_ref[...], preferred_element_type=F32)
    tm = p.shape[0]
    lane = lax.broadcasted_iota(jnp.int32, (tm, LANES), 1)

    fqT_ref[...] = (p[:, 0:512] * QK_SCALE).T.astype(BF16)
    fk = p[:, 512:1024]
    fv = p[:, 1024:1536]
    dqT_ref[...] = (p[:, 1536:2048] * QK_SCALE).T.astype(BF16)
    dk = p[:, 2048:2560]
    dv = p[:, 2560:3072]
    fvT = fv.T
    dvT = dv.T
    fk_ref[n_prev] = fk.T
    fv_ref[n_prev] = fvT
    for h in range(N_DIFF_HEADS):
        hs = slice(LANES * h, LANES * (h + 1))
        dk_ref[n_prev, pl.ds(h, tm, stride=N_DIFF_HEADS), :] = dk[:, hs]
        dv_ref[n_prev, pl.ds(h, tm, stride=N_DIFF_HEADS), :] = dv[:, hs]

    z = jnp.dot(xb, wf_ref[...], preferred_element_type=F32) + bf_ref[...]
    lf = jnp.minimum(z, 0.0) - jnp.log1p(jnp.exp(-jnp.abs(z)))
    lf = jnp.where(lane < 24, lf, 0.0)
    lf_ref[n_prev] = lf.T[0:N_FOX_HEADS, :]

    hi, mid, lo = _split3(lf)
    packed = jnp.where(lane < 8, hi, jnp.where(lane < 16, mid, lo)).astype(BF16)
    zc = jnp.dot(tri_ref[...], packed, preferred_element_type=F32)
    c = zc + pltpu.roll(zc, 120, 1) + pltpu.roll(zc, 112, 1) + carry_ref[...]
    carry_ref[...] = c[tm - 1:tm, :]

    c2 = jnp.where(lane < 8, c * LOG2E, 0.0)
    c2rep = c2 + pltpu.roll(c2, 8, 1) + pltpu.roll(c2, 16, 1)
    h3, m3, l3 = _split3(c2rep)
    parts = jnp.where(lane < 8, h3, jnp.where(lane < 16, m3, l3))
    kparts = pltpu.roll(-parts, 32, 1)
    cqb = jnp.where((lane >= 32) & (lane < 56), 1.0, parts)
    kb = jnp.where(lane < 24, 1.0, kparts).astype(BF16)
    cqbT_ref[...] = cqb.T.astype(BF16)

    rowi = lax.broadcasted_iota(jnp.int32, (LANES, tm), 0)
    for j in range(4):
        sl = slice(LANES * j, LANES * (j + 1))
        fkaug_ref[j, :, 0:LANES] = fk[:, sl].astype(BF16)
        fkaug_ref[j, :, LANES:2 * LANES] = kb
        vpT = fvT[sl, :]
        fvT_ref[2 * j, 0] = jnp.where(rowi < HEAD_DIM, vpT, 1.0).astype(BF16)
        fvT_ref[2 * j + 1, 0] = jnp.where(rowi >= HEAD_DIM, vpT, 1.0).astype(BF16)
        dkaug_ref[j, :, 0:LANES] = dk[:, sl].astype(BF16)
        dkaug_ref[j, :, LANES:2 * LANES] = dkb_ref[j]
        dvT_ref[j, 0, 0:LANES, :] = dvT[sl, :].astype(BF16)
        dvT_ref[j, 0, LANES:DIFF_V_ROWS, :] = jnp.ones((DIFF_V_ROWS - LANES, tm), BF16)


def _project_prompt(x, wm, wf, bfv, tri, dkb, prev, *, batch, seq_len, tm):
    n, d = x.shape
    nb = n // tm
    bps = seq_len // tm
    n_prev = 0 if prev is None else prev[0].shape[0]
    nl = n_prev + 1
    row = lambda i: (i, 0)
    row3 = lambda i: (0, i, 0)
    const = lambda i: (0, 0)
    tmaj = lambda i: (0, i // bps, 0, i % bps)

    def stacked_specs(k):
        return [pl.BlockSpec((k, None, FOX_WIDTH, tm), tmaj), pl.BlockSpec((k, None, FOX_WIDTH, tm), tmaj),
                pl.BlockSpec((k, None, N_FOX_HEADS, tm), tmaj),
                pl.BlockSpec((k, tm * N_DIFF_HEADS, LANES), row3),
                pl.BlockSpec((k, tm * N_DIFF_HEADS, LANES), row3)]

    out_shape = (
        jax.ShapeDtypeStruct((512, n), BF16),
        jax.ShapeDtypeStruct((4, n, 256), BF16),
        jax.ShapeDtypeStruct((8, nb, LANES, tm), BF16),
        jax.ShapeDtypeStruct((LANES, n), BF16),
        jax.ShapeDtypeStruct((512, n), BF16),
        jax.ShapeDtypeStruct((4, n, 256), BF16),
        jax.ShapeDtypeStruct((4, nb, DIFF_V_ROWS, tm), BF16),
        jax.ShapeDtypeStruct((nl, batch, FOX_WIDTH, seq_len), F32),
        jax.ShapeDtypeStruct((nl, batch, FOX_WIDTH, seq_len), F32),
        jax.ShapeDtypeStruct((nl, batch, N_FOX_HEADS, seq_len), F32),
        jax.ShapeDtypeStruct((nl, n * N_DIFF_HEADS, LANES), F32),
        jax.ShapeDtypeStruct((nl, n * N_DIFF_HEADS, LANES), F32),
    )
    colb = lambda i: (0, i)
    chunk = lambda i: (0, i, 0, 0)
    out_specs = tuple([
        pl.BlockSpec((512, tm), colb),
        pl.BlockSpec((4, tm, 256), row3),
        pl.BlockSpec((8, 1, LANES, tm), chunk),
        pl.BlockSpec((LANES, tm), colb),
        pl.BlockSpec((512, tm), colb),
        pl.BlockSpec((4, tm, 256), row3),
        pl.BlockSpec((4, 1, DIFF_V_ROWS, tm), chunk),
    ] + stacked_specs(nl))
    in_specs = [
        pl.BlockSpec((tm, d), row),
        pl.BlockSpec(wm.shape, const),
        pl.BlockSpec(wf.shape, const),
        pl.BlockSpec(bfv.shape, const),
        pl.BlockSpec(tri.shape, const),
        pl.BlockSpec((4, tm, 128), lambda i: (0, i % bps, 0)),
    ] + (stacked_specs(n_prev) if n_prev else [])
    return pl.pallas_call(
        functools.partial(_proj_kernel, blocks_per_seq=bps, n_prev=n_prev),
        grid=(nb,), in_specs=in_specs, out_specs=out_specs, out_shape=out_shape,
        scratch_shapes=[pltpu.VMEM((1, LANES), F32)],
        compiler_params=pltpu.CompilerParams(dimension_semantics=("arbitrary",),
                                             vmem_limit_bytes=VMEM_LIMIT),
        name="proj_prompt",
    )(x, wm, wf, bfv, tri, dkb, *(prev or ()))


def _online_softmax_step(kb, qaug_ref, m_ref, acc_ref, vts_per_chain, masked):
    chains = range(len(vts_per_chain))
    nsub = len(vts_per_chain[0])
    sub = kb.shape[0] // nsub
    sts = [[jnp.dot(kb[k * sub:(k + 1) * sub], qaug_ref[c], preferred_element_type=F32) for c in chains]
           for k in range(nsub)]
    m_cur = [m_ref[c] for c in chains]
    for k in range(nsub):
        if masked:
            key = lax.broadcasted_iota(jnp.int32, sts[k][0].shape, 0) + k * sub
            qry = lax.broadcasted_iota(jnp.int32, sts[k][0].shape, 1)
            sts[k] = [jnp.where(key <= qry, st, NEG) for st in sts[k]]
        m_new = [jnp.maximum(m_cur[c], jnp.max(sts[k][c], axis=0, keepdims=True)) for c in chains]
        pts = [jnp.exp2(sts[k][c] - m_new[c]).astype(BF16) for c in chains]
        for c in chains:
            new = jnp.dot(vts_per_chain[c][k], pts[c], preferred_element_type=F32)
            acc_ref[c] = jnp.exp2(m_cur[c] - m_new[c]) * acc_ref[c] + new
        m_cur = m_new
    for c in chains:
        m_ref[c] = m_cur[c]


BLOCKS_PER_TRIP = 4


def _causal_sweep(qi, step):
    def body(p, carry):
        step(BLOCKS_PER_TRIP * p, BLOCKS_PER_TRIP, False)
        return carry
    lax.fori_loop(0, qi // BLOCKS_PER_TRIP, body, 0)

    done = (qi // BLOCKS_PER_TRIP) * BLOCKS_PER_TRIP
    size = BLOCKS_PER_TRIP // 2
    while size >= 1:
        here = done

        @pl.when((qi & size) != 0)
        def _():
            step(here, size, False)

        done = done + (qi & size)
        size //= 2
    step(qi, 1, True)


def _fox_attn_kernel(qT_ref, cqbT_ref, k_ref, vT_ref, o_ref, qaug_ref, acc_ref, m_ref, *, blk, nsub):
    pair = pl.program_id(1)
    qi = pl.program_id(2)
    rowi = lax.broadcasted_iota(jnp.int32, (LANES, blk), 0)
    qT = qT_ref[...].astype(F32)
    cT = cqbT_ref[...].astype(F32)
    bias_rows = (rowi < 24) | ((rowi >= 32) & (rowi < 56))
    for hh in range(2):
        h = 2 * pair + hh
        in_head = (rowi >= HEAD_DIM * hh) & (rowi < HEAD_DIM * (hh + 1))
        sel = ((rowi & 7) == h) & bias_rows
        qaug_ref[hh, 0:LANES, :] = jnp.where(in_head, qT, 0.0).astype(BF16)
        qaug_ref[hh, LANES:2 * LANES, :] = jnp.where(sel, cT, 0.0).astype(BF16)
    m_ref[...] = jnp.full(m_ref.shape, NEG, F32)
    acc_ref[...] = jnp.zeros(acc_ref.shape, F32)

    def step(ki, nblk, masked):
        start = pl.multiple_of(ki * blk, blk)
        kb = k_ref[pl.ds(start, nblk * blk), :]
        vts = [[vT_ref[hh, ki * nsub + c] for c in range(nblk * nsub)] for hh in range(2)]
        _online_softmax_step(kb, qaug_ref, m_ref, acc_ref, vts, masked)

    _causal_sweep(qi, step)
    a0 = acc_ref[0]
    a1 = acc_ref[1]
    o0 = a0 / a0[HEAD_DIM:HEAD_DIM + 1, :]
    o1 = a1 / a1[0:1, :]
    o_ref[...] = jnp.where(rowi < HEAD_DIM, o0, o1).T.astype(o_ref.dtype)


def _fox_attention(fqT, cqbT, fkaug, fvT, *, batch, seq_len, blk):
    n = fqT.shape[1]
    nq = seq_len // blk
    sub = fvT.shape[-1]
    return pl.pallas_call(
        functools.partial(_fox_attn_kernel, blk=blk, nsub=blk // sub),
        grid=(batch, 4, nq),
        in_specs=[
            pl.BlockSpec((LANES, blk), lambda b, p, i: (p, b * nq + i)),
            pl.BlockSpec((LANES, blk), lambda b, p, i: (0, b * nq + i)),
            pl.BlockSpec((None, seq_len, 256), lambda b, p, i: (p, b, 0)),
            pl.BlockSpec((2, seq_len // sub, LANES, sub), lambda b, p, i: (p, b, 0, 0)),
        ],
        out_specs=pl.BlockSpec((blk, LANES), lambda b, p, i: (b * nq + i, p)),
        out_shape=jax.ShapeDtypeStruct((n, FOX_WIDTH), BF16),
        scratch_shapes=[pltpu.VMEM((2, 256, blk), BF16), pltpu.VMEM((2, LANES, blk), F32),
                        pltpu.VMEM((2, 1, blk), F32)],
        compiler_params=pltpu.CompilerParams(
            dimension_semantics=("arbitrary", "arbitrary", "arbitrary"), vmem_limit_bytes=VMEM_LIMIT),
        name="fox_attn",
    )(fqT, cqbT, fkaug, fvT)


def _diff_lambda(lq1, lk1, lq2, lk2, lambda_init):
    return (jnp.exp(jnp.sum(lq1 * lk1, axis=1, keepdims=True))
            - jnp.exp(jnp.sum(lq2 * lk2, axis=1, keepdims=True)) + lambda_init)


def _diff_combine(o1, o2, lam, g, lambda_init, axis):
    a = o1 - lam * o2
    a = a * lax.rsqrt(jnp.mean(a * a, axis=axis, keepdims=True) + RMS_EPS)
    return a * g * (1.0 - lambda_init)


def _diff_attn_kernel(qT_ref, qbT_ref, k_ref, vT_ref, lq1_ref, lk1_ref, lq2_ref, lk2_ref, g_ref,
                      o_ref, qaug_ref, acc_ref, m_ref, *, blk, nsub, lambda_init):
    qi = pl.program_id(2)
    rowi = lax.broadcasted_iota(jnp.int32, (LANES, blk), 0)
    qT = qT_ref[...].astype(F32)
    for c in range(2):
        half = (rowi >= HEAD_DIM * c) & (rowi < HEAD_DIM * (c + 1))
        qaug_ref[c, 0:LANES, :] = jnp.where(half, qT, 0.0).astype(BF16)
        qaug_ref[c, LANES:2 * LANES, :] = qbT_ref[...]
    m_ref[...] = jnp.full(m_ref.shape, NEG, F32)
    acc_ref[...] = jnp.zeros(acc_ref.shape, F32)

    def step(ki, nblk, masked):
        start = pl.multiple_of(ki * blk, blk)
        kb = k_ref[pl.ds(start, nblk * blk), :]
        vts = [vT_ref[ki * nsub + c] for c in range(nblk * nsub)]
        _online_softmax_step(kb, qaug_ref, m_ref, acc_ref, [vts, vts], masked)

    _causal_sweep(qi, step)
    a1 = acc_ref[0]
    a2 = acc_ref[1]
    o1 = a1[0:LANES] / a1[LANES:LANES + 1]
    o2 = a2[0:LANES] / a2[LANES:LANES + 1]
    lam = _diff_lambda(lq1_ref[...], lk1_ref[...], lq2_ref[...], lk2_ref[...], lambda_init)
    o_ref[...] = _diff_combine(o1, o2, lam, g_ref[...], lambda_init, 0).T.astype(o_ref.dtype)


def _diff_attention(dqT, dqbT, dkaug, dvT, lq1, lk1, lq2, lk2, gcol, *, batch, seq_len, blk, lambda_init):
    n = dqT.shape[1]
    nq = seq_len // blk
    sub = dvT.shape[-1]
    vrows = dvT.shape[-2]
    vec = lambda b, h, i: (0, 0)
    return pl.pallas_call(
        functools.partial(_diff_attn_kernel, blk=blk, nsub=blk // sub, lambda_init=lambda_init),
        grid=(batch, N_DIFF_HEADS, nq),
        in_specs=[
            pl.BlockSpec((LANES, blk), lambda b, h, i: (h, b * nq + i)),
            pl.BlockSpec((None, LANES, blk), lambda b, h, i: (h, 0, i)),
            pl.BlockSpec((None, seq_len, 256), lambda b, h, i: (h, b, 0)),
            pl.BlockSpec((None, seq_len // sub, vrows, sub), lambda b, h, i: (h, b, 0, 0)),
            pl.BlockSpec((1, HEAD_DIM), vec), pl.BlockSpec((1, HEAD_DIM), vec),
            pl.BlockSpec((1, HEAD_DIM), vec), pl.BlockSpec((1, HEAD_DIM), vec),
            pl.BlockSpec((LANES, 1), vec),
        ],
        out_specs=pl.BlockSpec((blk, LANES), lambda b, h, i: (b * nq + i, h)),
        out_shape=jax.ShapeDtypeStruct((n, DIFF_WIDTH), BF16),
        scratch_shapes=[pltpu.VMEM((2, 256, blk), BF16), pltpu.VMEM((2, vrows, blk), F32),
                        pltpu.VMEM((2, 1, blk), F32)],
        compiler_params=pltpu.CompilerParams(
            dimension_semantics=("arbitrary", "arbitrary", "arbitrary"), vmem_limit_bytes=VMEM_LIMIT),
        name="diff_attn",
    )(dqT, dqbT, dkaug, dvT, lq1, lk1, lq2, lk2, gcol)


def _route(logits):
    lg = logits.T[0:N_EXPERTS]
    rowi = lax.broadcasted_iota(jnp.int32, lg.shape, 0)
    rowf = rowi.astype(F32)
    mx = jnp.max(lg, axis=0, keepdims=True)
    e = jnp.exp(lg - mx)
    scores = e / jnp.sum(e, axis=0, keepdims=True)
    best = None
    for g in range(N_GROUPS):
        ing = (rowi >= EXPERTS_PER_GROUP * g) & (rowi < EXPERTS_PER_GROUP * (g + 1))
        sg = jnp.where(ing, scores, -1.0)
        m1 = jnp.max(sg, axis=0, keepdims=True)
        i1 = jnp.min(jnp.where(sg == m1, rowf, 999.0), axis=0, keepdims=True)
        sg2 = jnp.where(rowf == i1, -1.0, sg)
        m2 = jnp.max(sg2, axis=0, keepdims=True)
        i2 = jnp.min(jnp.where(sg2 == m2, rowf, 999.0), axis=0, keepdims=True)
        cand = (m1 + m2, m1, m2, i1, i2)
        if best is None:
            best = cand
        else:
            upd = cand[0] > best[0]
            best = tuple(jnp.where(upd, cn, bs) for cn, bs in zip(cand, best))
    _, v1, v2, i1, i2 = best
    den = v1 + v2
    comb_t = jnp.where(rowf == i1, v1 / den, 0.0) + jnp.where(rowf == i2, v2 / den, 0.0)
    pad = jnp.zeros((LANES - N_EXPERTS, comb_t.shape[1]), F32)
    return jnp.concatenate([comb_t, pad], axis=0).T


def _outproj_kernel(fox_ref, diff_ref, h_ref, wo_ref, g_ref, b_ref, wr_ref, br_ref,
                    h1_ref, h1b_ref, comb_ref, *, dn_alpha):
    mixed = jnp.concatenate([fox_ref[...], diff_ref[...]], axis=1)
    a = jnp.dot(mixed, wo_ref[...], preferred_element_type=F32)
    h1 = _layer_norm(dn_alpha * h_ref[...] + a, g_ref[...], b_ref[...])
    h1_ref[...] = h1
    x_hi = h1.astype(BF16)
    h1b_ref[...] = x_hi
    x_lo = (h1 - x_hi.astype(F32)).astype(BF16)
    r1 = jnp.dot(x_hi, wr_ref[...], preferred_element_type=F32)
    r2 = jnp.dot(x_lo, wr_ref[...], preferred_element_type=F32)
    logits = r1 + pltpu.roll(r1, 112, 1) + r2 + br_ref[...]
    comb_ref[...] = _route(logits)


def _out_proj(fox, diff, h, wo, g, b, wr, br, *, tm, dn_alpha):
    n, d = h.shape
    row = lambda i: (i, 0)
    const = lambda i: (0, 0)
    return pl.pallas_call(
        functools.partial(_outproj_kernel, dn_alpha=dn_alpha),
        grid=(n // tm,),
        in_specs=[
            pl.BlockSpec((tm, 512), row), pl.BlockSpec((tm, 512), row), pl.BlockSpec((tm, d), row),
            pl.BlockSpec(wo.shape, const), pl.BlockSpec((1, d), const), pl.BlockSpec((1, d), const),
            pl.BlockSpec(wr.shape, const), pl.BlockSpec((1, LANES), const),
        ],
        out_specs=(pl.BlockSpec((tm, d), row), pl.BlockSpec((tm, d), row), pl.BlockSpec((tm, LANES), row)),
        out_shape=(jax.ShapeDtypeStruct((n, d), F32), jax.ShapeDtypeStruct((n, d), BF16),
                   jax.ShapeDtypeStruct((n, LANES), F32)),
        compiler_params=pltpu.CompilerParams(dimension_semantics=("arbitrary",),
                                             vmem_limit_bytes=VMEM_LIMIT),
        name="out_proj_ln_router",
    )(fox, diff, h, wo, g, b, wr, br)


def _moe_kernel(x_ref, h1_ref, comb_ref, wg_ref, wu_ref, wd_ref, g_ref, b_ref, o_ref, acc_ref,
                *, dn_alpha):
    e = pl.program_id(1)

    @pl.when(e == 0)
    def _():
        acc_ref[...] = jnp.zeros_like(acc_ref)

    x = x_ref[...]
    hg = jnp.dot(x, wg_ref[...], preferred_element_type=F32)
    hu = jnp.dot(x, wu_ref[...], preferred_element_type=F32)
    comb = comb_ref[...]
    lane = lax.broadcasted_iota(jnp.int32, comb.shape, 1)
    ce = jnp.sum(jnp.where(lane == e, comb, 0.0), axis=1, keepdims=True)
    act = hg * jax.nn.sigmoid(hg) * hu * ce
    acc_ref[...] += jnp.dot(act.astype(BF16), wd_ref[...], preferred_element_type=F32)

    @pl.when(e == N_EXPERTS - 1)
    def _():
        o_ref[...] = _layer_norm(dn_alpha * h1_ref[...] + acc_ref[...], g_ref[...], b_ref[...])


def _moe(xb, h1, comb, wg, wu, wd, g, b, *, layer, tm, dn_alpha):
    n, d = h1.shape
    f = wg.shape[-1]
    row = lambda i, e: (i, 0)
    const = lambda i, e: (0, 0)
    return pl.pallas_call(
        functools.partial(_moe_kernel, dn_alpha=dn_alpha),
        grid=(n // tm, N_EXPERTS),
        in_specs=[
            pl.BlockSpec((tm, d), row), pl.BlockSpec((tm, d), row), pl.BlockSpec((tm, LANES), row),
            pl.BlockSpec((None, None, d, f), lambda i, e: (layer, e, 0, 0)),
            pl.BlockSpec((None, None, d, f), lambda i, e: (layer, e, 0, 0)),
            pl.BlockSpec((None, None, f, d), lambda i, e: (layer, e, 0, 0)),
            pl.BlockSpec((1, d), const), pl.BlockSpec((1, d), const),
        ],
        out_specs=pl.BlockSpec((tm, d), row),
        out_shape=jax.ShapeDtypeStruct((n, d), F32),
        scratch_shapes=[pltpu.VMEM((tm, d), F32)],
        compiler_params=pltpu.CompilerParams(dimension_semantics=("arbitrary", "arbitrary"),
                                             vmem_limit_bytes=VMEM_LIMIT),
        name="moe_ln",
    )(xb, h1, comb, wg, wu, wd, g, b)


def _t_dot(a, b):
    return lax.dot_general(a, b, (((0,), (0,)), ((), ())), preferred_element_type=F32)


def _moe_grouped_kernel(x_ref, h1_ref, comb_ref, wg_ref, wu_ref, wd_ref, g_ref, b_ref, tri_ref, o_ref,
                        acc_ref, sel_ref, xg_ref, gates_ref, yg_ref, memb_ref, rank_ref, cnt_ref,
                        *, dn_alpha, cap):
    e = pl.program_id(1)
    grp = e // EXPERTS_PER_GROUP
    tm = x_ref.shape[0]
    lane = lax.broadcasted_iota(jnp.int32, (cap, LANES), 1)

    @pl.when(e == 0)
    def _():
        acc_ref[...] = jnp.zeros_like(acc_ref)
        comb_t = comb_ref[...].T
        rows = []
        for gg in range(N_GROUPS):
            w = jnp.sum(comb_t[EXPERTS_PER_GROUP * gg:EXPERTS_PER_GROUP * (gg + 1)], axis=0, keepdims=True)
            rows.append(jnp.where(w > 0.0, 1.0, 0.0))
        memb = jnp.concatenate(rows + [jnp.zeros((8 - N_GROUPS, tm), F32)], axis=0)
        memb_ref[...] = memb
        rank_ref[...] = jnp.dot(memb.astype(BF16), tri_ref[...], preferred_element_type=F32)
        for gg in range(N_GROUPS):
            cnt_ref[gg] = jnp.sum(rows[gg]).astype(jnp.int32)

    def select(chunk):
        slot = lax.broadcasted_iota(jnp.int32, (cap, tm), 0).astype(F32) + (chunk * cap).astype(F32)
        hit = (rank_ref[pl.ds(grp, 1), :] == slot) & (memb_ref[pl.ds(grp, 1), :] > 0.0)
        return jnp.where(hit, 1.0, 0.0).astype(BF16)

    def gather(sel):
        xg = jnp.dot(sel, x_ref[...], preferred_element_type=F32).astype(BF16)
        comb = comb_ref[...]
        c_hi = _round8(comb)
        c_lo = _round8(comb - c_hi)
        gates = (jnp.dot(sel, c_hi.astype(BF16), preferred_element_type=F32)
                 + jnp.dot(sel, c_lo.astype(BF16), preferred_element_type=F32))
        return xg, gates

    def expert(xg, gates):
        hg = jnp.dot(xg, wg_ref[...], preferred_element_type=F32)
        hu = jnp.dot(xg, wu_ref[...], preferred_element_type=F32)
        ge = jnp.sum(jnp.where(lane == e, gates, 0.0), axis=1, keepdims=True)
        act = hg * jax.nn.sigmoid(hg) * hu * ge
        return jnp.dot(act.astype(BF16), wd_ref[...], preferred_element_type=F32)

    def scatter(sel, y):
        acc_ref[...] += _t_dot(sel, y.astype(BF16))

    @pl.when(e % EXPERTS_PER_GROUP == 0)
    def _():
        sel = select(jnp.int32(0))
        sel_ref[...] = sel
        xg, gates = gather(sel)
        xg_ref[...] = xg
        gates_ref[...] = gates
        yg_ref[...] = jnp.zeros_like(yg_ref)

    yg_ref[...] += expert(xg_ref[...], gates_ref[...])

    def extra_chunk(chunk, carry):
        sel = select(chunk)
        xg, gates = gather(sel)
        scatter(sel, expert(xg, gates))
        return carry

    lax.fori_loop(1, (cnt_ref[grp] + cap - 1) // cap, extra_chunk, 0)

    @pl.when(e % EXPERTS_PER_GROUP == EXPERTS_PER_GROUP - 1)
    def _():
        scatter(sel_ref[...], yg_ref[...])

    @pl.when(e == N_EXPERTS - 1)
    def _():
        o_ref[...] = _layer_norm(dn_alpha * h1_ref[...] + acc_ref[...], g_ref[...], b_ref[...])


def _moe_grouped(xb, h1, comb, wg, wu, wd, g, b, *, layer, tm, cap, dn_alpha):
    n, d = h1.shape
    f = wg.shape[-1]
    row = lambda i, e: (i, 0)
    const = lambda i, e: (0, 0)
    t = jnp.arange(tm)
    tri = (t[:, None] < t[None, :]).astype(BF16)
    return pl.pallas_call(
        functools.partial(_moe_grouped_kernel, dn_alpha=dn_alpha, cap=cap),
        grid=(n // tm, N_EXPERTS),
        in_specs=[
            pl.BlockSpec((tm, d), row), pl.BlockSpec((tm, d), row), pl.BlockSpec((tm, LANES), row),
            pl.BlockSpec((None, None, d, f), lambda i, e: (layer, e, 0, 0)),
            pl.BlockSpec((None, None, d, f), lambda i, e: (layer, e, 0, 0)),
            pl.BlockSpec((None, None, f, d), lambda i, e: (layer, e, 0, 0)),
            pl.BlockSpec((1, d), const), pl.BlockSpec((1, d), const),
            pl.BlockSpec((tm, tm), const),
        ],
        out_specs=pl.BlockSpec((tm, d), row),
        out_shape=jax.ShapeDtypeStruct((n, d), F32),
        scratch_shapes=[pltpu.VMEM((tm, d), F32), pltpu.VMEM((cap, tm), BF16), pltpu.VMEM((cap, d), BF16),
                        pltpu.VMEM((cap, LANES), F32), pltpu.VMEM((cap, d), F32),
                        pltpu.VMEM((8, tm), F32), pltpu.VMEM((8, tm), F32), pltpu.SMEM((N_GROUPS,), jnp.int32)],
        compiler_params=pltpu.CompilerParams(dimension_semantics=("arbitrary", "arbitrary"),
                                             vmem_limit_bytes=VMEM_LIMIT),
        name="moe_grouped_ln",
    )(xb, h1, comb, wg, wu, wd, g, b, tri)


def _proj_sample_kernel(x_ref, wm_ref, wf_ref, bf_ref, p_ref, lf_ref):
    xb = x_ref[...].astype(BF16)
    p = jnp.dot(xb, wm_ref[...], preferred_element_type=F32)
    col = lax.broadcasted_iota(jnp.int32, p.shape, 1)
    is_q = (col < 512) | ((col >= 1536) & (col < 2048))
    p_ref[...] = jnp.where(is_q, p * QK_SCALE, p)
    z = jnp.dot(xb, wf_ref[...], preferred_element_type=F32) + bf_ref[...]
    lf_ref[...] = jnp.minimum(z, 0.0) - jnp.log1p(jnp.exp(-jnp.abs(z)))


def _project_sample(x, wm, wf, bfv):
    n = x.shape[0]
    return pl.pallas_call(
        _proj_sample_kernel,
        out_shape=(jax.ShapeDtypeStruct((n, wm.shape[1]), F32), jax.ShapeDtypeStruct((n, LANES), F32)),
        compiler_params=pltpu.CompilerParams(vmem_limit_bytes=VMEM_LIMIT),
        name="proj_sample",
    )(x, wm, wf, bfv)


def _rows_per_head(x, rows):
    return jnp.concatenate([jnp.broadcast_to(x[h:h + 1, :], (rows, x.shape[1])) for h in range(x.shape[0])],
                           axis=0)


def _decode_attn_kernel(pt_ref, *refs, group, page, n_pages, lambda_init):
    g = group
    fkc = refs[0:g]
    fvc = refs[g:2 * g]
    lfc = refs[2 * g:3 * g]
    dkc = refs[3 * g:4 * g]
    dvc = refs[4 * g:5 * g]
    (fqc_ref, fq_ref, fkn_ref, fvnc_ref, lfn_ref, tri_ref, dq_ref, dkn_ref, dvn_ref,
     lq1_ref, lk1_ref, lq2_ref, lk2_ref, g_ref,
     fo_ref, do_ref,
     qb_ref, mf_ref, lf_ref, accf_ref, carry_ref, md_ref, ld_ref, accd_ref) = refs[5 * g:]
    j = pl.program_id(1)
    hd = N_FOX_HEADS * HEAD_DIM

    @pl.when(j == 0)
    def _():
        qb_ref[...] = jnp.broadcast_to(fqc_ref[...], (hd, LANES))
        carry_ref[...] = lfn_ref[...]
        mf_ref[...] = jnp.full(mf_ref.shape, NEG, F32)
        md_ref[...] = jnp.full(md_ref.shape, NEG, F32)
        lf_ref[...] = jnp.zeros_like(lf_ref)
        ld_ref[...] = jnp.zeros_like(ld_ref)
        accf_ref[...] = jnp.zeros_like(accf_ref)
        accd_ref[...] = jnp.zeros_like(accd_ref)

    dq4 = dq_ref[...]
    lane4 = lax.broadcasted_iota(jnp.int32, dq4.shape, 1)
    dq8 = jnp.concatenate([jnp.where(lane4 < HEAD_DIM, dq4, 0.0), jnp.where(lane4 >= HEAD_DIM, dq4, 0.0)], axis=0)
    dq8_hi = dq8.astype(BF16)
    dq16 = jnp.concatenate([dq8_hi, (dq8 - dq8_hi.astype(F32)).astype(BF16)], axis=0)
    nd = page * N_DIFF_HEADS
    rowd = lax.broadcasted_iota(jnp.int32, (8, nd), 0)
    cold = lax.broadcasted_iota(jnp.int32, (8, nd), 1)
    hrow = rowd & (N_DIFF_HEADS - 1)
    slope2 = jnp.exp2(-(8.0 / N_DIFF_HEADS) * (hrow.astype(F32) + 1.0)) * LOG2E
    keep = (cold & (N_DIFF_HEADS - 1)) == hrow
    t_in_page = cold // N_DIFF_HEADS
    sd_list = []
    for i in range(g):
        first_pos = (n_pages - 1 - (j * g + i)) * page
        dist = (n_pages * page - first_pos - t_in_page).astype(F32)
        sd2 = _nt_dot(dq16, dkc[i][...].astype(BF16))
        sd = sd2[0:8] + sd2[8:16]
        sd_list.append(jnp.where(keep, sd - slope2 * dist, NEG))

    qb = qb_ref[...]
    tri = tri_ref[...]
    carry = carry_ref[...]
    s_list = []
    for i in range(g):
        s = jnp.sum((fkc[i][...] * qb).reshape(N_FOX_HEADS, HEAD_DIM, LANES), axis=1)
        lfp = lfc[i][...] * LOG2E
        within = sum(jnp.dot(t.astype(BF16), tri, preferred_element_type=F32) for t in _split3(lfp))
        s_list.append(s + within + carry)
        carry = carry + jnp.sum(lfp, axis=1, keepdims=True)
    carry_ref[...] = carry
    m_prev = mf_ref[...]
    m_new = m_prev
    for s in s_list:
        m_new = jnp.maximum(m_new, jnp.max(s, axis=1, keepdims=True))
    alpha = jnp.exp2(m_prev - m_new)
    l_new = alpha * lf_ref[...]
    acc = _rows_per_head(jnp.broadcast_to(alpha, (N_FOX_HEADS, LANES)), HEAD_DIM) * accf_ref[...]
    for i in range(g):
        p = jnp.exp2(s_list[i] - m_new)
        l_new = l_new + jnp.sum(p, axis=1, keepdims=True)
        acc = acc + _rows_per_head(p, HEAD_DIM) * fvc[i][...]
    accf_ref[...] = acc
    lf_ref[...] = l_new
    mf_ref[...] = m_new

    m_prev = md_ref[...]
    m_new = m_prev
    for sd in sd_list:
        m_new = jnp.maximum(m_new, jnp.max(sd, axis=1, keepdims=True))
    alpha = jnp.exp2(m_prev - m_new)
    l_new = alpha * ld_ref[...]
    accd = alpha * accd_ref[...]
    for i in range(g):
        p = jnp.exp2(sd_list[i] - m_new)
        l_new = l_new + jnp.sum(p, axis=1, keepdims=True)
        p_hi = p.astype(BF16)
        p_lo = (p - p_hi.astype(F32)).astype(BF16)
        pv = jnp.dot(jnp.concatenate([p_hi, p_lo], axis=0), dvc[i][...].astype(BF16), preferred_element_type=F32)
        accd = accd + pv[0:8] + pv[8:16]
    accd_ref[...] = accd
    ld_ref[...] = l_new
    md_ref[...] = m_new

    @pl.when(j == pl.num_programs(1) - 1)
    def _():
        fq = fq_ref[...]
        s_self = jnp.sum(fq * fkn_ref[...], axis=1, keepdims=True)
        m_prev = mf_ref[...]
        m_fin = jnp.maximum(m_prev, s_self)
        a = jnp.exp2(m_prev - m_fin)
        p_self = jnp.exp2(s_self - m_fin)
        l_fin = a * lf_ref[...] + p_self
        num = jnp.sum(accf_ref[...], axis=1, keepdims=True)
        fo_ref[...] = ((_rows_per_head(a, HEAD_DIM) * num + _rows_per_head(p_self, HEAD_DIM) * fvnc_ref[...])
                       / _rows_per_head(l_fin, HEAD_DIM))

        dk8 = jnp.concatenate([dkn_ref[...], dkn_ref[...]], axis=0)
        dv8 = jnp.concatenate([dvn_ref[...], dvn_ref[...]], axis=0)
        s_self = jnp.sum(dq8 * dk8, axis=1, keepdims=True)
        m_prev = md_ref[...]
        m_fin = jnp.maximum(m_prev, s_self)
        a = jnp.exp2(m_prev - m_fin)
        p_self = jnp.exp2(s_self - m_fin)
        od = (a * accd_ref[...] + p_self * dv8) / (a * ld_ref[...] + p_self)
        lam = _diff_lambda(lq1_ref[...], lk1_ref[...], lq2_ref[...], lk2_ref[...], lambda_init)
        do_ref[...] = _diff_combine(od[0:N_DIFF_HEADS], od[N_DIFF_HEADS:], lam, g_ref[...], lambda_init, -1)


def _decode_attention(page_table, layer, fkc, fvc, lfc, dkc, dvc, fq, fkn, fvn, lfn, dq, dkn, dvn,
                      lq1, lk1, lq2, lk2, gain, *, group, lambda_init):
    nseq, n_pages = page_table.shape
    page = fkc.shape[-1]
    hd = N_FOX_HEADS * HEAD_DIM
    seq3 = lambda b, j, pt: (b, 0, 0)
    vec = lambda b, j, pt: (0, 0)

    def pool(i):
        return lambda b, j, pt: (layer, pt[b, n_pages - 1 - (j * group + i)], 0, 0)

    def paged(arr):
        return [pl.BlockSpec((None, None) + arr.shape[2:], pool(i)) for i in range(group)]

    tri = (jnp.arange(page)[:, None] > jnp.arange(page)[None, :]).astype(BF16)
    f3 = lambda a: a.reshape(nseq, N_FOX_HEADS, HEAD_DIM)
    d3 = lambda a: a.reshape(nseq, N_DIFF_HEADS, 2 * HEAD_DIM)
    col = lambda a: a.reshape(nseq, -1, 1)
    in_specs = (paged(fkc) + paged(fvc) + paged(lfc) + paged(dkc) + paged(dvc) + [
        pl.BlockSpec((None, hd, 1), seq3),
        pl.BlockSpec((None, N_FOX_HEADS, HEAD_DIM), seq3),
        pl.BlockSpec((None, N_FOX_HEADS, HEAD_DIM), seq3),
        pl.BlockSpec((None, hd, 1), seq3),
        pl.BlockSpec((None, N_FOX_HEADS, 1), seq3),
        pl.BlockSpec((page, page), vec),
        pl.BlockSpec((None, N_DIFF_HEADS, LANES), seq3),
        pl.BlockSpec((None, N_DIFF_HEADS, LANES), seq3),
        pl.BlockSpec((None, N_DIFF_HEADS, LANES), seq3),
        pl.BlockSpec((1, HEAD_DIM), vec), pl.BlockSpec((1, HEAD_DIM), vec),
        pl.BlockSpec((1, HEAD_DIM), vec), pl.BlockSpec((1, HEAD_DIM), vec),
        pl.BlockSpec((1, LANES), vec),
    ])
    args = ([fkc] * group + [fvc] * group + [lfc] * group + [dkc] * group + [dvc] * group
            + [col(fq), f3(fq), f3(fkn), col(fvn), col(lfn), tri, d3(dq), d3(dkn), d3(dvn),
               lq1, lk1, lq2, lk2, gain])
    fo, do = pl.pallas_call(
        functools.partial(_decode_attn_kernel, group=group, page=page, n_pages=n_pages, lambda_init=lambda_init),
        grid_spec=pltpu.PrefetchScalarGridSpec(
            num_scalar_prefetch=1, grid=(nseq, n_pages // group), in_specs=in_specs,
            out_specs=(pl.BlockSpec((None, hd, 1), seq3),
                       pl.BlockSpec((None, N_DIFF_HEADS, LANES), seq3)),
            scratch_shapes=[pltpu.VMEM((hd, LANES), F32),
                            pltpu.VMEM((8, 1), F32), pltpu.VMEM((8, 1), F32), pltpu.VMEM((hd, LANES), F32),
                            pltpu.VMEM((8, 1), F32),
                            pltpu.VMEM((8, 1), F32), pltpu.VMEM((8, 1), F32), pltpu.VMEM((8, LANES), F32)]),
        out_shape=(jax.ShapeDtypeStruct((nseq, hd, 1), F32),
                   jax.ShapeDtypeStruct((nseq, N_DIFF_HEADS, LANES), F32)),
        compiler_params=pltpu.CompilerParams(dimension_semantics=("arbitrary", "arbitrary"),
                                             vmem_limit_bytes=VMEM_LIMIT),
        name="decode_attn",
    )(page_table, *args)
    return fo.reshape(nseq, hd), do.reshape(nseq, DIFF_WIDTH)


def _pad_lanes(x, width=LANES):
    return jnp.pad(x, [(0, 0)] * (x.ndim - 1) + [(0, width - x.shape[-1])])


def _alibi_tables(seq_len):
    pos = jnp.arange(seq_len, dtype=F32)
    slopes = jnp.asarray([2.0 ** (-8.0 * (i + 1) / N_DIFF_HEADS) for i in range(N_DIFF_HEADS)], F32)
    u = jnp.stack(_split3(slopes[:, None] * LOG2E * pos[None, :]), axis=-1)
    one = jnp.ones_like(u)
    qb = _pad_lanes(jnp.concatenate([-u, one], axis=-1)).astype(BF16)
    kb = _pad_lanes(jnp.concatenate([one, u], axis=-1)).astype(BF16)
    return qb, kb


def kernel(x_prompt, x_sample, cache_fox_k, cache_fox_v, cache_fox_logf, cache_diff_k, cache_diff_v,
           page_table, w_in, b_forget, w_out, lambda_q1, lambda_k1, lambda_q2, lambda_k2, subln_g,
           ln1_g, ln1_b, ln2_g, ln2_b, w_router, b_router, w_gate, w_up, w_down):
    batch, seq_len, d_model = x_prompt.shape
    depth = w_in.shape[0]
    nseq = x_sample.shape[0]
    n_pool, page = cache_fox_k.shape[1], cache_fox_k.shape[2]
    n = batch * seq_len
    dn_alpha = (2 * depth) ** 0.25
    blk = min(512, seq_len)
    tm_proj = min(256, seq_len)
    tm_tok = min(512, n)
    tm_moe = min(1024, n)

    gate0 = 3 * FOX_WIDTH
    wm = jnp.concatenate([w_in[:, :, :gate0], w_in[:, :, gate0 + N_FOX_HEADS:]], axis=2).astype(BF16)
    wf = _pad_lanes(jnp.tile(w_in[:, :, gate0:gate0 + N_FOX_HEADS], (1, 1, 3))).astype(BF16)
    bfv = _pad_lanes(jnp.tile(b_forget, (1, 3)))[:, None, :]
    wo = w_out.astype(BF16)
    wr_hi = _round8(w_router)
    wr_lo = _round8(w_router - wr_hi)
    wr = _pad_lanes(jnp.concatenate([wr_hi, wr_lo], axis=1)).astype(BF16)
    br = _pad_lanes(b_router[None, :])
    wg = w_gate.astype(BF16)
    wu = w_up.astype(BF16)
    wd = w_down.astype(BF16)
    tri = jnp.tril(jnp.ones((tm_proj, tm_proj), BF16))
    dqb, dkb = _alibi_tables(seq_len)
    dqbT = jnp.transpose(dqb, (0, 2, 1))

    fkc = jnp.transpose(cache_fox_k, (0, 1, 3, 4, 2)).reshape(depth, n_pool, FOX_WIDTH, page)
    fvc = jnp.transpose(cache_fox_v, (0, 1, 3, 4, 2)).reshape(depth, n_pool, FOX_WIDTH, page)
    lfc = jnp.transpose(cache_fox_logf, (0, 1, 3, 2))
    dkc = cache_diff_k.reshape(depth, n_pool, page * N_DIFF_HEADS, 2 * HEAD_DIM)
    dvc = cache_diff_v.reshape(depth, n_pool, page * N_DIFF_HEADS, 2 * HEAD_DIM)
    group = math.gcd(8, page_table.shape[1])

    hp = x_prompt.reshape(n, d_model)
    hs = x_sample.reshape(nseq, d_model)
    p_new = None
    s_rows = [[] for _ in range(5)]
    for layer in range(depth):
        lambda_init = 0.8 - 0.6 * math.exp(-0.3 * layer)
        lam_args = (lambda_q1[layer][None], lambda_k1[layer][None], lambda_q2[layer][None],
                    lambda_k2[layer][None], subln_g[layer][None])
        ln1 = (ln1_g[layer][None], ln1_b[layer][None])
        ln2 = (ln2_g[layer][None], ln2_b[layer][None])

        (fqT, fkaug, fvT, cqbT, dqT, dkaug, dvT, *p_new) = _project_prompt(
            hp, wm[layer], wf[layer], bfv[layer], tri, dkb, p_new, batch=batch, seq_len=seq_len, tm=tm_proj)
        fox = _fox_attention(fqT, cqbT, fkaug, fvT, batch=batch, seq_len=seq_len, blk=blk)
        diff = _diff_attention(dqT, dqbT, dkaug, dvT, *lam_args[:4], subln_g[layer][:, None], batch=batch,
                               seq_len=seq_len, blk=blk, lambda_init=lambda_init)
        h1, h1b, comb = _out_proj(fox, diff, hp, wo[layer], *ln1, wr, br, tm=tm_tok, dn_alpha=dn_alpha)
        hp = _moe_grouped(h1b, h1, comb, wg, wu, wd, *ln2, layer=layer, tm=tm_moe,
                          cap=min(MOE_CAP, tm_moe), dn_alpha=dn_alpha)

        ps, lfs = _project_sample(hs, wm[layer], wf[layer], bfv[layer])
        sfq, sfk, sfv, sdq, sdk, sdv = (ps[:, 512 * c:512 * (c + 1)] for c in range(6))
        lfs = lfs[:, :N_FOX_HEADS]
        sfox, sdiff = _decode_attention(page_table, layer, fkc, fvc, lfc, dkc, dvc, sfq, sfk, sfv, lfs * LOG2E,
                                        sdq, sdk, sdv, *lam_args, group=group, lambda_init=lambda_init)
        sfox = sfox.astype(BF16)
        sdiff = sdiff.astype(BF16)
        s1, s1b, scomb = _out_proj(sfox, sdiff, hs, wo[layer], *ln1, wr, br, tm=nseq, dn_alpha=dn_alpha)
        hs = _moe(s1b, s1, scomb, wg, wu, wd, *ln2, layer=layer, tm=nseq, dn_alpha=dn_alpha)
        for lst, val in zip(s_rows, (
                sfk.reshape(nseq, 1, N_FOX_HEADS, HEAD_DIM), sfv.reshape(nseq, 1, N_FOX_HEADS, HEAD_DIM),
                lfs.reshape(nseq, 1, N_FOX_HEADS),
                sdk.reshape(nseq, 1, N_DIFF_HEADS, 2 * HEAD_DIM), sdv.reshape(nseq, 1, N_DIFF_HEADS, 2 * HEAD_DIM))):
            lst.append(val)

    fk_t, fv_t, lf_t, dk_s, dv_s = p_new
    heads_t = lambda a: jnp.transpose(a.reshape(depth, batch, N_FOX_HEADS, HEAD_DIM, seq_len), (0, 1, 4, 2, 3))
    outs_p = [heads_t(fk_t), heads_t(fv_t), jnp.transpose(lf_t, (0, 1, 3, 2)),
              dk_s.reshape(depth, batch, seq_len, N_DIFF_HEADS, 2 * HEAD_DIM),
              dv_s.reshape(depth, batch, seq_len, N_DIFF_HEADS, 2 * HEAD_DIM)]
    outs_s = [jnp.stack(r, 0) for r in s_rows]
    return (hp.reshape(batch, seq_len, d_model), hs.reshape(nseq, 1, d_model), *outs_p, *outs_s)
```

```python
import functools
import math

import jax
import jax.numpy as jnp
from jax import lax
from jax.experimental import pallas as pl
from jax.experimental.pallas import tpu as pltpu

F32 = jnp.float32
BF16 = jnp.bfloat16

HEAD_DIM = 64
N_FOX_HEADS = 8
N_DIFF_HEADS = 4
FOX_WIDTH = N_FOX_HEADS * HEAD_DIM
DIFF_WIDTH = N_DIFF_HEADS * 2 * HEAD_DIM
N_EXPERTS = 16
N_GROUPS = 4
EXPERTS_PER_GROUP = N_EXPERTS // N_GROUPS
LN_EPS = 1e-5
RMS_EPS = 1e-5
LOG2E = 1.4426950408889634
QK_SCALE = HEAD_DIM ** -0.5 * LOG2E
NEG = -1e30
LANES = 128
VMEM_LIMIT = 56 * 1024 * 1024
DIFF_V_ROWS = LANES + 16
MOE_CAP = 304


def _round8(x):
    t = x * 65537.0
    return t - (t - x)


def _split3(x):
    hi = _round8(x)
    r = x - hi
    mid = _round8(r)
    return hi, mid, r - mid


def _layer_norm(x, g, b):
    mu = jnp.mean(x, axis=-1, keepdims=True)
    xc = x - mu
    var = jnp.mean(xc * xc, axis=-1, keepdims=True)
    return xc * lax.rsqrt(var + LN_EPS) * g + b


def _nt_dot(a, b):
    return lax.dot_general(a, b, (((1,), (1,)), ((), ())), preferred_element_type=F32)


def _proj_kernel(*refs, blocks_per_seq, n_prev):
    x_ref, wm_ref, wf_ref, bf_ref, tri_ref, dkb_ref = refs[0:6]
    prev = refs[6:6 + 5] if n_prev else ()
    (fqT_ref, fkaug_ref, fvT_ref, cqbT_ref, dqT_ref, dkaug_ref, dvT_ref,
     fk_ref, fv_ref, lf_ref, dk_ref, dv_ref, carry_ref) = refs[6 + len(prev):]
    i = pl.program_id(0)
    for src, dst in zip(prev, (fk_ref, fv_ref, lf_ref, dk_ref, dv_ref)):
        for l in range(n_prev):
            dst[l] = src[l]

    @pl.when(i % blocks_per_seq == 0)
    def _():
        carry_ref[...] = jnp.zeros_like(carry_ref)

    xb = x_ref[...].astype(BF16)
    p = jnp.dot(xb, wm_ref[...], preferred_element_type=F32)
    tm = p.shape[0]
    lane = lax.broadcasted_iota(jnp.int32, (tm, LANES), 1)

    fqT_ref[...] = (p[:, 0:512] * QK_SCALE).T.astype(BF16)
    fk = p[:, 512:1024]
    fv = p[:, 1024:1536]
    dqT_ref[...] = (p[:, 1536:2048] * QK_SCALE).T.astype(BF16)
    dk = p[:, 2048:2560]
    dv = p[:, 2560:3072]
    fvT = fv.T
    dvT = dv.T
    fk_ref[n_prev] = fk.T
    fv_ref[n_prev] = fvT
    for h in range(N_DIFF_HEADS):
        hs = slice(LANES * h, LANES * (h + 1))
        dk_ref[n_prev, pl.ds(h, tm, stride=N_DIFF_HEADS), :] = dk[:, hs]
        dv_ref[n_prev, pl.ds(h, tm, stride=N_DIFF_HEADS), :] = dv[:, hs]

    z = jnp.dot(xb, wf_ref[...], preferred_element_type=F32) + bf_ref[...]
    lf = jnp.minimum(z, 0.0) - jnp.log1p(jnp.exp(-jnp.abs(z)))
    lf = jnp.where(lane < 24, lf, 0.0)
    lf_ref[n_prev] = lf.T[0:N_FOX_HEADS, :]

    hi, mid, lo = _split3(lf)
    packed = jnp.where(lane < 8, hi, jnp.where(lane < 16, mid, lo)).astype(BF16)
    zc = jnp.dot(tri_ref[...], packed, preferred_element_type=F32)
    c = zc + pltpu.roll(zc, 120, 1) + pltpu.roll(zc, 112, 1) + carry_ref[...]
    carry_ref[...] = c[tm - 1:tm, :]

    c2 = jnp.where(lane < 8, c * LOG2E, 0.0)
    c2rep = c2 + pltpu.roll(c2, 8, 1) + pltpu.roll(c2, 16, 1)
    h3, m3, l3 = _split3(c2rep)
    parts = jnp.where(lane < 8, h3, jnp.where(lane < 16, m3, l3))
    kparts = pltpu.roll(-parts, 32, 1)
    cqb = jnp.where((lane >= 32) & (lane < 56), 1.0, parts)
    kb = jnp.where(lane < 24, 1.0, kparts).astype(BF16)
    cqbT_ref[...] = cqb.T.astype(BF16)

    rowi = lax.broadcasted_iota(jnp.int32, (LANES, tm), 0)
    for j in range(4):
        sl = slice(LANES * j, LANES * (j + 1))
        fkaug_ref[j, :, 0:LANES] = fk[:, sl].astype(BF16)
        fkaug_ref[j, :, LANES:2 * LANES] = kb
        vpT = fvT[sl, :]
        fvT_ref[2 * j, 0] = jnp.where(rowi < HEAD_DIM, vpT, 1.0).astype(BF16)
        fvT_ref[2 * j + 1, 0] = jnp.where(rowi >= HEAD_DIM, vpT, 1.0).astype(BF16)
        dkaug_ref[j, :, 0:LANES] = dk[:, sl].astype(BF16)
        dkaug_ref[j, :, LANES:2 * LANES] = dkb_ref[j]
        dvT_ref[j, 0, 0:LANES, :] = dvT[sl, :].astype(BF16)
        dvT_ref[j, 0, LANES:DIFF_V_ROWS, :] = jnp.ones((DIFF_V_ROWS - LANES, tm), BF16)


def _project_prompt(x, wm, wf, bfv, tri, dkb, prev, *, batch, seq_len, tm):
    n, d = x.shape
    nb = n // tm
    bps = seq_len // tm
    n_prev = 0 if prev is None else prev[0].shape[0]
    nl = n_prev + 1
    row = lambda i: (i, 0)
    row3 = lambda i: (0, i, 0)
    const = lambda i: (0, 0)
    tmaj = lambda i: (0, i // bps, 0, i % bps)

    def stacked_specs(k):
        return [pl.BlockSpec((k, None, FOX_WIDTH, tm), tmaj), pl.BlockSpec((k, None, FOX_WIDTH, tm), tmaj),
                pl.BlockSpec((k, None, N_FOX_HEADS, tm), tmaj),
                pl.BlockSpec((k, tm * N_DIFF_HEADS, LANES), row3),
                pl.BlockSpec((k, tm * N_DIFF_HEADS, LANES), row3)]

    out_shape = (
        jax.ShapeDtypeStruct((512, n), BF16),
        jax.ShapeDtypeStruct((4, n, 256), BF16),
        jax.ShapeDtypeStruct((8, nb, LANES, tm), BF16),
        jax.ShapeDtypeStruct((LANES, n), BF16),
        jax.ShapeDtypeStruct((512, n), BF16),
        jax.ShapeDtypeStruct((4, n, 256), BF16),
        jax.ShapeDtypeStruct((4, nb, DIFF_V_ROWS, tm), BF16),
        jax.ShapeDtypeStruct((nl, batch, FOX_WIDTH, seq_len), F32),
        jax.ShapeDtypeStruct((nl, batch, FOX_WIDTH, seq_len), F32),
        jax.ShapeDtypeStruct((nl, batch, N_FOX_HEADS, seq_len), F32),
        jax.ShapeDtypeStruct((nl, n * N_DIFF_HEADS, LANES), F32),
        jax.ShapeDtypeStruct((nl, n * N_DIFF_HEADS, LANES), F32),
    )
    colb = lambda i: (0, i)
    chunk = lambda i: (0, i, 0, 0)
    out_specs = tuple([
        pl.BlockSpec((512, tm), colb),
        pl.BlockSpec((4, tm, 256), row3),
        pl.BlockSpec((8, 1, LANES, tm), chunk),
        pl.BlockSpec((LANES, tm), colb),
        pl.BlockSpec((512, tm), colb),
        pl.BlockSpec((4, tm, 256), row3),
        pl.BlockSpec((4, 1, DIFF_V_ROWS, tm), chunk),
    ] + stacked_specs(nl))
    in_specs = [
        pl.BlockSpec((tm, d), row),
        pl.BlockSpec(wm.shape, const),
        pl.BlockSpec(wf.shape, const),
        pl.BlockSpec(bfv.shape, const),
        pl.BlockSpec(tri.shape, const),
        pl.BlockSpec((4, tm, 128), lambda i: (0, i % bps, 0)),
    ] + (stacked_specs(n_prev) if n_prev else [])
    return pl.pallas_call(
        functools.partial(_proj_kernel, blocks_per_seq=bps, n_prev=n_prev),
        grid=(nb,), in_specs=in_specs, out_specs=out_specs, out_shape=out_shape,
        scratch_shapes=[pltpu.VMEM((1, LANES), F32)],
        compiler_params=pltpu.CompilerParams(dimension_semantics=("arbitrary",),
                                             vmem_limit_bytes=VMEM_LIMIT),
        name="proj_prompt",
    )(x, wm, wf, bfv, tri, dkb, *(prev or ()))


def _online_softmax_step(kb, qaug_ref, m_ref, acc_ref, vts_per_chain, masked):
    chains = range(len(vts_per_chain))
    nsub = len(vts_per_chain[0])
    sub = kb.shape[0] // nsub
    sts = [[jnp.dot(kb[k * sub:(k + 1) * sub], qaug_ref[c], preferred_element_type=F32) for c in chains]
           for k in range(nsub)]
    m_cur = [m_ref[c] for c in chains]
    for k in range(nsub):
        if masked:
            key = lax.broadcasted_iota(jnp.int32, sts[k][0].shape, 0) + k * sub
            qry = lax.broadcasted_iota(jnp.int32, sts[k][0].shape, 1)
            sts[k] = [jnp.where(key <= qry, st, NEG) for st in sts[k]]
        m_new = [jnp.maximum(m_cur[c], jnp.max(sts[k][c], axis=0, keepdims=True)) for c in chains]
        pts = [jnp.exp2(sts[k][c] - m_new[c]).astype(BF16) for c in chains]
        for c in chains:
            new = jnp.dot(vts_per_chain[c][k], pts[c], preferred_element_type=F32)
            acc_ref[c] = jnp.exp2(m_cur[c] - m_new[c]) * acc_ref[c] + new
        m_cur = m_new
    for c in chains:
        m_ref[c] = m_cur[c]


BLOCKS_PER_TRIP = 4


def _causal_sweep(qi, step):
    def body(p, carry):
        step(BLOCKS_PER_TRIP * p, BLOCKS_PER_TRIP, False)
        return carry
    lax.fori_loop(0, qi // BLOCKS_PER_TRIP, body, 0)

    done = (qi // BLOCKS_PER_TRIP) * BLOCKS_PER_TRIP
    size = BLOCKS_PER_TRIP // 2
    while size >= 1:
        here = done

        @pl.when((qi & size) != 0)
        def _():
            step(here, size, False)

        done = done + (qi & size)
        size //= 2
    step(qi, 1, True)


def _fox_attn_kernel(qT_ref, cqbT_ref, k_ref, vT_ref, o_ref, qaug_ref, acc_ref, m_ref, *, blk, nsub):
    pair = pl.program_id(1)
    qi = pl.program_id(2)
    rowi = lax.broadcasted_iota(jnp.int32, (LANES, blk), 0)
    qT = qT_ref[...].astype(F32)
    cT = cqbT_ref[...].astype(F32)
    bias_rows = (rowi < 24) | ((rowi >= 32) & (rowi < 56))
    for hh in range(2):
        h = 2 * pair + hh
        in_head = (rowi >= HEAD_DIM * hh) & (rowi < HEAD_DIM * (hh + 1))
        sel = ((rowi & 7) == h) & bias_rows
        qaug_ref[hh, 0:LANES, :] = jnp.where(in_head, qT, 0.0).astype(BF16)
        qaug_ref[hh, LANES:2 * LANES, :] = jnp.where(sel, cT, 0.0).astype(BF16)
    m_ref[...] = jnp.full(m_ref.shape, NEG, F32)
    acc_ref[...] = jnp.zeros(acc_ref.shape, F32)

    def step(ki, nblk, masked):
        start = pl.multiple_of(ki * blk, blk)
        kb = k_ref[pl.ds(start, nblk * blk), :]
        vts = [[vT_ref[hh, ki * nsub + c] for c in range(nblk * nsub)] for hh in range(2)]
        _online_softmax_step(kb, qaug_ref, m_ref, acc_ref, vts, masked)

    _causal_sweep(qi, step)
    a0 = acc_ref[0]
    a1 = acc_ref[1]
    o0 = a0 / a0[HEAD_DIM:HEAD_DIM + 1, :]
    o1 = a1 / a1[0:1, :]
    o_ref[...] = jnp.where(rowi < HEAD_DIM, o0, o1).T.astype(o_ref.dtype)


def _fox_attention(fqT, cqbT, fkaug, fvT, *, batch, seq_len, blk):
    n = fqT.shape[1]
    nq = seq_len // blk
    sub = fvT.shape[-1]
    return pl.pallas_call(
        functools.partial(_fox_attn_kernel, blk=blk, nsub=blk // sub),
        grid=(batch, 4, nq),
        in_specs=[
            pl.BlockSpec((LANES, blk), lambda b, p, i: (p, b * nq + i)),
            pl.BlockSpec((LANES, blk), lambda b, p, i: (0, b * nq + i)),
            pl.BlockSpec((None, seq_len, 256), lambda b, p, i: (p, b, 0)),
            pl.BlockSpec((2, seq_len // sub, LANES, sub), lambda b, p, i: (p, b, 0, 0)),
        ],
        out_specs=pl.BlockSpec((blk, LANES), lambda b, p, i: (b * nq + i, p)),
        out_shape=jax.ShapeDtypeStruct((n, FOX_WIDTH), BF16),
        scratch_shapes=[pltpu.VMEM((2, 256, blk), BF16), pltpu.VMEM((2, LANES, blk), F32),
                        pltpu.VMEM((2, 1, blk), F32)],
        compiler_params=pltpu.CompilerParams(
            dimension_semantics=("arbitrary", "arbitrary", "arbitrary"), vmem_limit_bytes=VMEM_LIMIT),
        name="fox_attn",
    )(fqT, cqbT, fkaug, fvT)


def _diff_lambda(lq1, lk1, lq2, lk2, lambda_init):
    return (jnp.exp(jnp.sum(lq1 * lk1, axis=1, keepdims=True))
            - jnp.exp(jnp.sum(lq2 * lk2, axis=1, keepdims=True)) + lambda_init)


def _diff_combine(o1, o2, lam, g, lambda_init, axis):
    a = o1 - lam * o2
    a = a * lax.rsqrt(jnp.mean(a * a, axis=axis, keepdims=True) + RMS_EPS)
    return a * g * (1.0 - lambda_init)


def _diff_attn_kernel(qT_ref, qbT_ref, k_ref, vT_ref, lq1_ref, lk1_ref, lq2_ref, lk2_ref, g_ref,
                      o_ref, qaug_ref, acc_ref, m_ref, *, blk, nsub, lambda_init):
    qi = pl.program_id(2)
    rowi = lax.broadcasted_iota(jnp.int32, (LANES, blk), 0)
    qT = qT_ref[...].astype(F32)
    for c in range(2):
        half = (rowi >= HEAD_DIM * c) & (rowi < HEAD_DIM * (c + 1))
        qaug_ref[c, 0:LANES, :] = jnp.where(half, qT, 0.0).astype(BF16)
        qaug_ref[c, LANES:2 * LANES, :] = qbT_ref[...]
    m_ref[...] = jnp.full(m_ref.shape, NEG, F32)
    acc_ref[...] = jnp.zeros(acc_ref.shape, F32)

    def step(ki, nblk, masked):
        start = pl.multiple_of(ki * blk, blk)
        kb = k_ref[pl.ds(start, nblk * blk), :]
        vts = [vT_ref[ki * nsub + c] for c in range(nblk * nsub)]
        _online_softmax_step(kb, qaug_ref, m_ref, acc_ref, [vts, vts], masked)

    _causal_sweep(qi, step)
    a1 = acc_ref[0]
    a2 = acc_ref[1]
    o1 = a1[0:LANES] / a1[LANES:LANES + 1]
    o2 = a2[0:LANES] / a2[LANES:LANES + 1]
    lam = _diff_lambda(lq1_ref[...], lk1_ref[...], lq2_ref[...], lk2_ref[...], lambda_init)
    o_ref[...] = _diff_combine(o1, o2, lam, g_ref[...], lambda_init, 0).T.astype(o_ref.dtype)


def _diff_attention(dqT, dqbT, dkaug, dvT, lq1, lk1, lq2, lk2, gcol, *, batch, seq_len, blk, lambda_init):
    n = dqT.shape[1]
    nq = seq_len // blk
    sub = dvT.shape[-1]
    vrows = dvT.shape[-2]
    vec = lambda b, h, i: (0, 0)
    return pl.pallas_call(
        functools.partial(_diff_attn_kernel, blk=blk, nsub=blk // sub, lambda_init=lambda_init),
        grid=(batch, N_DIFF_HEADS, nq),
        in_specs=[
            pl.BlockSpec((LANES, blk), lambda b, h, i: (h, b * nq + i)),
            pl.BlockSpec((None, LANES, blk), lambda b, h, i: (h, 0, i)),
            pl.BlockSpec((None, seq_len, 256), lambda b, h, i: (h, b, 0)),
            pl.BlockSpec((None, seq_len // sub, vrows, sub), lambda b, h, i: (h, b, 0, 0)),
            pl.BlockSpec((1, HEAD_DIM), vec), pl.BlockSpec((1, HEAD_DIM), vec),
            pl.BlockSpec((1, HEAD_DIM), vec), pl.BlockSpec((1, HEAD_DIM), vec),
            pl.BlockSpec((LANES, 1), vec),
        ],
        out_specs=pl.BlockSpec((blk, LANES), lambda b, h, i: (b * nq + i, h)),
        out_shape=jax.ShapeDtypeStruct((n, DIFF_WIDTH), BF16),
        scratch_shapes=[pltpu.VMEM((2, 256, blk), BF16), pltpu.VMEM((2, vrows, blk), F32),
                        pltpu.VMEM((2, 1, blk), F32)],
        compiler_params=pltpu.CompilerParams(
            dimension_semantics=("arbitrary", "arbitrary", "arbitrary"), vmem_limit_bytes=VMEM_LIMIT),
        name="diff_attn",
    )(dqT, dqbT, dkaug, dvT, lq1, lk1, lq2, lk2, gcol)


def _route(logits):
    lg = logits.T[0:N_EXPERTS]
    rowi = lax.broadcasted_iota(jnp.int32, lg.shape, 0)
    rowf = rowi.astype(F32)
    mx = jnp.max(lg, axis=0, keepdims=True)
    e = jnp.exp(lg - mx)
    scores = e / jnp.sum(e, axis=0, keepdims=True)
    best = None
    for g in range(N_GROUPS):
        ing = (rowi >= EXPERTS_PER_GROUP * g) & (rowi < EXPERTS_PER_GROUP * (g + 1))
        sg = jnp.where(ing, scores, -1.0)
        m1 = jnp.max(sg, axis=0, keepdims=True)
        i1 = jnp.min(jnp.where(sg == m1, rowf, 999.0), axis=0, keepdims=True)
        sg2 = jnp.where(rowf == i1, -1.0, sg)
        m2 = jnp.max(sg2, axis=0, keepdims=True)
        i2 = jnp.min(jnp.where(sg2 == m2, rowf, 999.0), axis=0, keepdims=True)
        cand = (m1 + m2, m1, m2, i1, i2)
        if best is None:
            best = cand
        else:
            upd = cand[0] > best[0]
            best = tuple(jnp.where(upd, cn, bs) for cn, bs in zip(cand, best))
    _, v1, v2, i1, i2 = best
    den = v1 + v2
    comb_t = jnp.where(rowf == i1, v1 / den, 0.0) + jnp.where(rowf == i2, v2 / den, 0.0)
    pad = jnp.zeros((LANES - N_EXPERTS, comb_t.shape[1]), F32)
    return jnp.concatenate([comb_t, pad], axis=0).T


def _outproj_kernel(fox_ref, diff_ref, h_ref, wo_ref, g_ref, b_ref, wr_ref, br_ref,
                    h1_ref, h1b_ref, comb_ref, *, dn_alpha):
    mixed = jnp.concatenate([fox_ref[...], diff_ref[...]], axis=1)
    a = jnp.dot(mixed, wo_ref[...], preferred_element_type=F32)
    h1 = _layer_norm(dn_alpha * h_ref[...] + a, g_ref[...], b_ref[...])
    h1_ref[...] = h1
    x_hi = h1.astype(BF16)
    h1b_ref[...] = x_hi
    x_lo = (h1 - x_hi.astype(F32)).astype(BF16)
    r1 = jnp.dot(x_hi, wr_ref[...], preferred_element_type=F32)
    r2 = jnp.dot(x_lo, wr_ref[...], preferred_element_type=F32)
    logits = r1 + pltpu.roll(r1, 112, 1) + r2 + br_ref[...]
    comb_ref[...] = _route(logits)


def _out_proj(fox, diff, h, wo, g, b, wr, br, *, tm, dn_alpha):
    n, d = h.shape
    row = lambda i: (i, 0)
    const = lambda i: (0, 0)
    return pl.pallas_call(
        functools.partial(_outproj_kernel, dn_alpha=dn_alpha),
        grid=(n // tm,),
        in_specs=[
            pl.BlockSpec((tm, 512), row), pl.BlockSpec((tm, 512), row), pl.BlockSpec((tm, d), row),
            pl.BlockSpec(wo.shape, const), pl.BlockSpec((1, d), const), pl.BlockSpec((1, d), const),
            pl.BlockSpec(wr.shape, const), pl.BlockSpec((1, LANES), const),
        ],
        out_specs=(pl.BlockSpec((tm, d), row), pl.BlockSpec((tm, d), row), pl.BlockSpec((tm, LANES), row)),
        out_shape=(jax.ShapeDtypeStruct((n, d), F32), jax.ShapeDtypeStruct((n, d), BF16),
                   jax.ShapeDtypeStruct((n, LANES), F32)),
        compiler_params=pltpu.CompilerParams(dimension_semantics=("arbitrary",),
                                             vmem_limit_bytes=VMEM_LIMIT),
        name="out_proj_ln_router",
    )(fox, diff, h, wo, g, b, wr, br)


def _moe_kernel(x_ref, h1_ref, comb_ref, wg_ref, wu_ref, wd_ref, g_ref, b_ref, o_ref, acc_ref,
                *, dn_alpha):
    e = pl.program_id(1)

    @pl.when(e == 0)
    def _():
        acc_ref[...] = jnp.zeros_like(acc_ref)

    x = x_ref[...]
    hg = jnp.dot(x, wg_ref[...], preferred_element_type=F32)
    hu = jnp.dot(x, wu_ref[...], preferred_element_type=F32)
    comb = comb_ref[...]
    lane = lax.broadcasted_iota(jnp.int32, comb.shape, 1)
    ce = jnp.sum(jnp.where(lane == e, comb, 0.0), axis=1, keepdims=True)
    act = hg * jax.nn.sigmoid(hg) * hu * ce
    acc_ref[...] += jnp.dot(act.astype(BF16), wd_ref[...], preferred_element_type=F32)

    @pl.when(e == N_EXPERTS - 1)
    def _():
        o_ref[...] = _layer_norm(dn_alpha * h1_ref[...] + acc_ref[...], g_ref[...], b_ref[...])


def _moe(xb, h1, comb, wg, wu, wd, g, b, *, layer, tm, dn_alpha):
    n, d = h1.shape
    f = wg.shape[-1]
    row = lambda i, e: (i, 0)
    const = lambda i, e: (0, 0)
    return pl.pallas_call(
        functools.partial(_moe_kernel, dn_alpha=dn_alpha),
        grid=(n // tm, N_EXPERTS),
        in_specs=[
            pl.BlockSpec((tm, d), row), pl.BlockSpec((tm, d), row), pl.BlockSpec((tm, LANES), row),
            pl.BlockSpec((None, None, d, f), lambda i, e: (layer, e, 0, 0)),
            pl.BlockSpec((None, None, d, f), lambda i, e: (layer, e, 0, 0)),
            pl.BlockSpec((None, None, f, d), lambda i, e: (layer, e, 0, 0)),
            pl.BlockSpec((1, d), const), pl.BlockSpec((1, d), const),
        ],
        out_specs=pl.BlockSpec((tm, d), row),
        out_shape=jax.ShapeDtypeStruct((n, d), F32),
        scratch_shapes=[pltpu.VMEM((tm, d), F32)],
        compiler_params=pltpu.CompilerParams(dimension_semantics=("arbitrary", "arbitrary"),
                                             vmem_limit_bytes=VMEM_LIMIT),
        name="moe_ln",
    )(xb, h1, comb, wg, wu, wd, g, b)


def _t_dot(a, b):
    return lax.dot_general(a, b, (((0,), (0,)), ((), ())), preferred_element_type=F32)


def _moe_grouped_kernel(x_ref, h1_ref, comb_ref, wg_ref, wu_ref, wd_ref, g_ref, b_ref, tri_ref, o_ref,
                        acc_ref, sel_ref, xg_ref, gates_ref, yg_ref, memb_ref, rank_ref, cnt_ref,
                        *, dn_alpha, cap):
    e = pl.program_id(1)
    grp = e // EXPERTS_PER_GROUP
    tm = x_ref.shape[0]
    lane = lax.broadcasted_iota(jnp.int32, (cap, LANES), 1)

    @pl.when(e == 0)
    def _():
        acc_ref[...] = jnp.zeros_like(acc_ref)
        comb_t = comb_ref[...].T
        rows = []
        for gg in range(N_GROUPS):
            w = jnp.sum(comb_t[EXPERTS_PER_GROUP * gg:EXPERTS_PER_GROUP * (gg + 1)], axis=0, keepdims=True)
            rows.append(jnp.where(w > 0.0, 1.0, 0.0))
        memb = jnp.concatenate(rows + [jnp.zeros((8 - N_GROUPS, tm), F32)], axis=0)
        memb_ref[...] = memb
        rank_ref[...] = jnp.dot(memb.astype(BF16), tri_ref[...], preferred_element_type=F32)
        for gg in range(N_GROUPS):
            cnt_ref[gg] = jnp.sum(rows[gg]).astype(jnp.int32)

    def select(chunk):
        slot = lax.broadcasted_iota(jnp.int32, (cap, tm), 0).astype(F32) + (chunk * cap).astype(F32)
        hit = (rank_ref[pl.ds(grp, 1), :] == slot) & (memb_ref[pl.ds(grp, 1), :] > 0.0)
        return jnp.where(hit, 1.0, 0.0).astype(BF16)

    def gather(sel):
        xg = jnp.dot(sel, x_ref[...], preferred_element_type=F32).astype(BF16)
        comb = comb_ref[...]
        c_hi = _round8(comb)
        c_lo = _round8(comb - c_hi)
        gates = (jnp.dot(sel, c_hi.astype(BF16), preferred_element_type=F32)
                 + jnp.dot(sel, c_lo.astype(BF16), preferred_element_type=F32))
        return xg, gates

    def expert(xg, gates):
        hg = jnp.dot(xg, wg_ref[...], preferred_element_type=F32)
        hu = jnp.dot(xg, wu_ref[...], preferred_element_type=F32)
        ge = jnp.sum(jnp.where(lane == e, gates, 0.0), axis=1, keepdims=True)
        act = hg * jax.nn.sigmoid(hg) * hu * ge
        return jnp.dot(act.astype(BF16), wd_ref[...], preferred_element_type=F32)

    def scatter(sel, y):
        acc_ref[...] += _t_dot(sel, y.astype(BF16))

    @pl.when(e % EXPERTS_PER_GROUP == 0)
    def _():
        sel = select(jnp.int32(0))
        sel_ref[...] = sel
        xg, gates = gather(sel)
        xg_ref[...] = xg
        gates_ref[...] = gates
        yg_ref[...] = jnp.zeros_like(yg_ref)

    yg_ref[...] += expert(xg_ref[...], gates_ref[...])

    def extra_chunk(chunk, carry):
        sel = select(chunk)
        xg, gates = gather(sel)
        scatter(sel, expert(xg, gates))
        return carry

    lax.fori_loop(1, (cnt_ref[grp] + cap - 1) // cap, extra_chunk, 0)

    @pl.when(e % EXPERTS_PER_GROUP == EXPERTS_PER_GROUP - 1)
    def _():
        scatter(sel_ref[...], yg_ref[...])

    @pl.when(e == N_EXPERTS - 1)
    def _():
        o_ref[...] = _layer_norm(dn_alpha * h1_ref[...] + acc_ref[...], g_ref[...], b_ref[...])


def _moe_grouped(xb, h1, comb, wg, wu, wd, g, b, *, layer, tm, cap, dn_alpha):
    n, d = h1.shape
    f = wg.shape[-1]
    row = lambda i, e: (i, 0)
    const = lambda i, e: (0, 0)
    t = jnp.arange(tm)
    tri = (t[:, None] < t[None, :]).astype(BF16)
    return pl.pallas_call(
        functools.partial(_moe_grouped_kernel, dn_alpha=dn_alpha, cap=cap),
        grid=(n // tm, N_EXPERTS),
        in_specs=[
            pl.BlockSpec((tm, d), row), pl.BlockSpec((tm, d), row), pl.BlockSpec((tm, LANES), row),
            pl.BlockSpec((None, None, d, f), lambda i, e: (layer, e, 0, 0)),
            pl.BlockSpec((None, None, d, f), lambda i, e: (layer, e, 0, 0)),
            pl.BlockSpec((None, None, f, d), lambda i, e: (layer, e, 0, 0)),
            pl.BlockSpec((1, d), const), pl.BlockSpec((1, d), const),
            pl.BlockSpec((tm, tm), const),
        ],
        out_specs=pl.BlockSpec((tm, d), row),
        out_shape=jax.ShapeDtypeStruct((n, d), F32),
        scratch_shapes=[pltpu.VMEM((tm, d), F32), pltpu.VMEM((cap, tm), BF16), pltpu.VMEM((cap, d), BF16),
                        pltpu.VMEM((cap, LANES), F32), pltpu.VMEM((cap, d), F32),
                        pltpu.VMEM((8, tm), F32), pltpu.VMEM((8, tm), F32), pltpu.SMEM((N_GROUPS,), jnp.int32)],
        compiler_params=pltpu.CompilerParams(dimension_semantics=("arbitrary", "arbitrary"),
                                             vmem_limit_bytes=VMEM_LIMIT),
        name="moe_grouped_ln",
    )(xb, h1, comb, wg, wu, wd, g, b, tri)


def _proj_sample_kernel(x_ref, wm_ref, wf_ref, bf_ref, p_ref, lf_ref):
    xb = x_ref[...].astype(BF16)
    p = jnp.dot(xb, wm_ref[...], preferred_element_type=F32)
    col = lax.broadcasted_iota(jnp.int32, p.shape, 1)
    is_q = (col < 512) | ((col >= 1536) & (col < 2048))
    p_ref[...] = jnp.where(is_q, p * QK_SCALE, p)
    z = jnp.dot(xb, wf_ref[...], preferred_element_type=F32) + bf_ref[...]
    lf_ref[...] = jnp.minimum(z, 0.0) - jnp.log1p(jnp.exp(-jnp.abs(z)))


def _project_sample(x, wm, wf, bfv):
    n = x.shape[0]
    return pl.pallas_call(
        _proj_sample_kernel,
        out_shape=(jax.ShapeDtypeStruct((n, wm.shape[1]), F32), jax.ShapeDtypeStruct((n, LANES), F32)),
        compiler_params=pltpu.CompilerParams(vmem_limit_bytes=VMEM_LIMIT),
        name="proj_sample",
    )(x, wm, wf, bfv)


def _rows_per_head(x, rows):
    return jnp.concatenate([jnp.broadcast_to(x[h:h + 1, :], (rows, x.shape[1])) for h in range(x.shape[0])],
                           axis=0)


def _decode_attn_kernel(pt_ref, *refs, group, page, n_pages, lambda_init):
    g = group
    fkc = refs[0:g]
    fvc = refs[g:2 * g]
    lfc = refs[2 * g:3 * g]
    dkc = refs[3 * g:4 * g]
    dvc = refs[4 * g:5 * g]
    (fqc_ref, fq_ref, fkn_ref, fvnc_ref, lfn_ref, tri_ref, dq_ref, dkn_ref, dvn_ref,
     lq1_ref, lk1_ref, lq2_ref, lk2_ref, g_ref,
     fo_ref, do_ref,
     qb_ref, mf_ref, lf_ref, accf_ref, carry_ref, md_ref, ld_ref, accd_ref) = refs[5 * g:]
    j = pl.program_id(1)
    hd = N_FOX_HEADS * HEAD_DIM

    @pl.when(j == 0)
    def _():
        qb_ref[...] = jnp.broadcast_to(fqc_ref[...], (hd, LANES))
        carry_ref[...] = lfn_ref[...]
        mf_ref[...] = jnp.full(mf_ref.shape, NEG, F32)
        md_ref[...] = jnp.full(md_ref.shape, NEG, F32)
        lf_ref[...] = jnp.zeros_like(lf_ref)
        ld_ref[...] = jnp.zeros_like(ld_ref)
        accf_ref[...] = jnp.zeros_like(accf_ref)
        accd_ref[...] = jnp.zeros_like(accd_ref)

    dq4 = dq_ref[...]
    lane4 = lax.broadcasted_iota(jnp.int32, dq4.shape, 1)
    dq8 = jnp.concatenate([jnp.where(lane4 < HEAD_DIM, dq4, 0.0), jnp.where(lane4 >= HEAD_DIM, dq4, 0.0)], axis=0)
    dq8_hi = dq8.astype(BF16)
    dq16 = jnp.concatenate([dq8_hi, (dq8 - dq8_hi.astype(F32)).astype(BF16)], axis=0)
    nd = page * N_DIFF_HEADS
    rowd = lax.broadcasted_iota(jnp.int32, (8, nd), 0)
    cold = lax.broadcasted_iota(jnp.int32, (8, nd), 1)
    hrow = rowd & (N_DIFF_HEADS - 1)
    slope2 = jnp.exp2(-(8.0 / N_DIFF_HEADS) * (hrow.astype(F32) + 1.0)) * LOG2E
    keep = (cold & (N_DIFF_HEADS - 1)) == hrow
    t_in_page = cold // N_DIFF_HEADS
    sd_list = []
    for i in range(g):
        first_pos = (n_pages - 1 - (j * g + i)) * page
        dist = (n_pages * page - first_pos - t_in_page).astype(F32)
        sd2 = _nt_dot(dq16, dkc[i][...].astype(BF16))
        sd = sd2[0:8] + sd2[8:16]
        sd_list.append(jnp.where(keep, sd - slope2 * dist, NEG))

    qb = qb_ref[...]
    tri = tri_ref[...]
    carry = carry_ref[...]
    s_list = []
    for i in range(g):
        s = jnp.sum((fkc[i][...] * qb).reshape(N_FOX_HEADS, HEAD_DIM, LANES), axis=1)
        lfp = lfc[i][...] * LOG2E
        within = sum(jnp.dot(t.astype(BF16), tri, preferred_element_type=F32) for t in _split3(lfp))
        s_list.append(s + within + carry)
        carry = carry + jnp.sum(lfp, axis=1, keepdims=True)
    carry_ref[...] = carry
    m_prev = mf_ref[...]
    m_new = m_prev
    for s in s_list:
        m_new = jnp.maximum(m_new, jnp.max(s, axis=1, keepdims=True))
    alpha = jnp.exp2(m_prev - m_new)
    l_new = alpha * lf_ref[...]
    acc = _rows_per_head(jnp.broadcast_to(alpha, (N_FOX_HEADS, LANES)), HEAD_DIM) * accf_ref[...]
    for i in range(g):
        p = jnp.exp2(s_list[i] - m_new)
        l_new = l_new + jnp.sum(p, axis=1, keepdims=True)
        acc = acc + _rows_per_head(p, HEAD_DIM) * fvc[i][...]
    accf_ref[...] = acc
    lf_ref[...] = l_new
    mf_ref[...] = m_new

    m_prev = md_ref[...]
    m_new = m_prev
    for sd in sd_list:
        m_new = jnp.maximum(m_new, jnp.max(sd, axis=1, keepdims=True))
    alpha = jnp.exp2(m_prev - m_new)
    l_new = alpha * ld_ref[...]
    accd = alpha * accd_ref[...]
    for i in range(g):
        p = jnp.exp2(sd_list[i] - m_new)
        l_new = l_new + jnp.sum(p, axis=1, keepdims=True)
        p_hi = p.astype(BF16)
        p_lo = (p - p_hi.astype(F32)).astype(BF16)
        pv = jnp.dot(jnp.concatenate([p_hi, p_lo], axis=0), dvc[i][...].astype(BF16), preferred_element_type=F32)
        accd = accd + pv[0:8] + pv[8:16]
    accd_ref[...] = accd
    ld_ref[...] = l_new
    md_ref[...] = m_new

    @pl.when(j == pl.num_programs(1) - 1)
    def _():
        fq = fq_ref[...]
        s_self = jnp.sum(fq * fkn_ref[...], axis=1, keepdims=True)
        m_prev = mf_ref[...]
        m_fin = jnp.maximum(m_prev, s_self)
        a = jnp.exp2(m_prev - m_fin)
        p_self = jnp.exp2(s_self - m_fin)
        l_fin = a * lf_ref[...] + p_self
        num = jnp.sum(accf_ref[...], axis=1, keepdims=True)
        fo_ref[...] = ((_rows_per_head(a, HEAD_DIM) * num + _rows_per_head(p_self, HEAD_DIM) * fvnc_ref[...])
                       / _rows_per_head(l_fin, HEAD_DIM))

        dk8 = jnp.concatenate([dkn_ref[...], dkn_ref[...]], axis=0)
        dv8 = jnp.concatenate([dvn_ref[...], dvn_ref[...]], axis=0)
        s_self = jnp.sum(dq8 * dk8, axis=1, keepdims=True)
        m_prev = md_ref[...]
        m_fin = jnp.maximum(m_prev, s_self)
        a = jnp.exp2(m_prev - m_fin)
        p_self = jnp.exp2(s_self - m_fin)
        od = (a * accd_ref[...] + p_self * dv8) / (a * ld_ref[...] + p_self)
        lam = _diff_lambda(lq1_ref[...], lk1_ref[...], lq2_ref[...], lk2_ref[...], lambda_init)
        do_ref[...] = _diff_combine(od[0:N_DIFF_HEADS], od[N_DIFF_HEADS:], lam, g_ref[...], lambda_init, -1)


def _decode_attention(page_table, layer, fkc, fvc, lfc, dkc, dvc, fq, fkn, fvn, lfn, dq, dkn, dvn,
                      lq1, lk1, lq2, lk2, gain, *, group, lambda_init):
    nseq, n_pages = page_table.shape
    page = fkc.shape[-1]
    hd = N_FOX_HEADS * HEAD_DIM
    seq3 = lambda b, j, pt: (b, 0, 0)
    vec = lambda b, j, pt: (0, 0)

    def pool(i):
        return lambda b, j, pt: (layer, pt[b, n_pages - 1 - (j * group + i)], 0, 0)

    def paged(arr):
        return [pl.BlockSpec((None, None) + arr.shape[2:], pool(i)) for i in range(group)]

    tri = (jnp.arange(page)[:, None] > jnp.arange(page)[None, :]).astype(BF16)
    f3 = lambda a: a.reshape(nseq, N_FOX_HEADS, HEAD_DIM)
    d3 = lambda a: a.reshape(nseq, N_DIFF_HEADS, 2 * HEAD_DIM)
    col = lambda a: a.reshape(nseq, -1, 1)
    in_specs = (paged(fkc) + paged(fvc) + paged(lfc) + paged(dkc) + paged(dvc) + [
        pl.BlockSpec((None, hd, 1), seq3),
        pl.BlockSpec((None, N_FOX_HEADS, HEAD_DIM), seq3),
        pl.BlockSpec((None, N_FOX_HEADS, HEAD_DIM), seq3),
        pl.BlockSpec((None, hd, 1), seq3),
        pl.BlockSpec((None, N_FOX_HEADS, 1), seq3),
        pl.BlockSpec((page, page), vec),
        pl.BlockSpec((None, N_DIFF_HEADS, LANES), seq3),
        pl.BlockSpec((None, N_DIFF_HEADS, LANES), seq3),
        pl.BlockSpec((None, N_DIFF_HEADS, LANES), seq3),
        pl.BlockSpec((1, HEAD_DIM), vec), pl.BlockSpec((1, HEAD_DIM), vec),
        pl.BlockSpec((1, HEAD_DIM), vec), pl.BlockSpec((1, HEAD_DIM), vec),
        pl.BlockSpec((1, LANES), vec),
    ])
    args = ([fkc] * group + [fvc] * group + [lfc] * group + [dkc] * group + [dvc] * group
            + [col(fq), f3(fq), f3(fkn), col(fvn), col(lfn), tri, d3(dq), d3(dkn), d3(dvn),
               lq1, lk1, lq2, lk2, gain])
    fo, do = pl.pallas_call(
        functools.partial(_decode_attn_kernel, group=group, page=page, n_pages=n_pages, lambda_init=lambda_init),
        grid_spec=pltpu.PrefetchScalarGridSpec(
            num_scalar_prefetch=1, grid=(nseq, n_pages // group), in_specs=in_specs,
            out_specs=(pl.BlockSpec((None, hd, 1), seq3),
                       pl.BlockSpec((None, N_DIFF_HEADS, LANES), seq3)),
            scratch_shapes=[pltpu.VMEM((hd, LANES), F32),
                            pltpu.VMEM((8, 1), F32), pltpu.VMEM((8, 1), F32), pltpu.VMEM((hd, LANES), F32),
                            pltpu.VMEM((8, 1), F32),
                            pltpu.VMEM((8, 1), F32), pltpu.VMEM((8, 1), F32), pltpu.VMEM((8, LANES), F32)]),
        out_shape=(jax.ShapeDtypeStruct((nseq, hd, 1), F32),
                   jax.ShapeDtypeStruct((nseq, N_DIFF_HEADS, LANES), F32)),
        compiler_params=pltpu.CompilerParams(dimension_semantics=("arbitrary", "arbitrary"),
                                             vmem_limit_bytes=VMEM_LIMIT),
        name="decode_attn",
    )(page_table, *args)
    return fo.reshape(nseq, hd), do.reshape(nseq, DIFF_WIDTH)


def _pad_lanes(x, width=LANES):
    return jnp.pad(x, [(0, 0)] * (x.ndim - 1) + [(0, width - x.shape[-1])])


def _alibi_tables(seq_len):
    pos = jnp.arange(seq_len, dtype=F32)
    slopes = jnp.asarray([2.0 ** (-8.0 * (i + 1) / N_DIFF_HEADS) for i in range(N_DIFF_HEADS)], F32)
    u = jnp.stack(_split3(slopes[:, None] * LOG2E * pos[None, :]), axis=-1)
    one = jnp.ones_like(u)
    qb = _pad_lanes(jnp.concatenate([-u, one], axis=-1)).astype(BF16)
    kb = _pad_lanes(jnp.concatenate([one, u], axis=-1)).astype(BF16)
    return qb, kb


def kernel(x_prompt, x_sample, cache_fox_k, cache_fox_v, cache_fox_logf, cache_diff_k, cache_diff_v,
           page_table, w_in, b_forget, w_out, lambda_q1, lambda_k1, lambda_q2, lambda_k2, subln_g,
           ln1_g, ln1_b, ln2_g, ln2_b, w_router, b_router, w_gate, w_up, w_down):
    batch, seq_len, d_model = x_prompt.shape
    depth = w_in.shape[0]
    nseq = x_sample.shape[0]
    n_pool, page = cache_fox_k.shape[1], cache_fox_k.shape[2]
    n = batch * seq_len
    dn_alpha = (2 * depth) ** 0.25
    blk = min(512, seq_len)
    tm_proj = min(256, seq_len)
    tm_tok = min(512, n)
    tm_moe = min(1024, n)

    gate0 = 3 * FOX_WIDTH
    wm = jnp.concatenate([w_in[:, :, :gate0], w_in[:, :, gate0 + N_FOX_HEADS:]], axis=2).astype(BF16)
    wf = _pad_lanes(jnp.tile(w_in[:, :, gate0:gate0 + N_FOX_HEADS], (1, 1, 3))).astype(BF16)
    bfv = _pad_lanes(jnp.tile(b_forget, (1, 3)))[:, None, :]
    wo = w_out.astype(BF16)
    wr_hi = _round8(w_router)
    wr_lo = _round8(w_router - wr_hi)
    wr = _pad_lanes(jnp.concatenate([wr_hi, wr_lo], axis=1)).astype(BF16)
    br = _pad_lanes(b_router[None, :])
    wg = w_gate.astype(BF16)
    wu = w_up.astype(BF16)
    wd = w_down.astype(BF16)
    tri = jnp.tril(jnp.ones((tm_proj, tm_proj), BF16))
    dqb, dkb = _alibi_tables(seq_len)
    dqbT = jnp.transpose(dqb, (0, 2, 1))

    fkc = jnp.transpose(cache_fox_k, (0, 1, 3, 4, 2)).reshape(depth, n_pool, FOX_WIDTH, page)
    fvc = jnp.transpose(cache_fox_v, (0, 1, 3, 4, 2)).reshape(depth, n_pool, FOX_WIDTH, page)
    lfc = jnp.transpose(cache_fox_logf, (0, 1, 3, 2))
    dkc = cache_diff_k.reshape(depth, n_pool, page * N_DIFF_HEADS, 2 * HEAD_DIM)
    dvc = cache_diff_v.reshape(depth, n_pool, page * N_DIFF_HEADS, 2 * HEAD_DIM)
    group = math.gcd(16, page_table.shape[1])

    hp = x_prompt.reshape(n, d_model)
    hs = x_sample.reshape(nseq, d_model)
    p_new = None
    s_rows = [[] for _ in range(5)]
    for layer in range(depth):
        lambda_init = 0.8 - 0.6 * math.exp(-0.3 * layer)
        lam_args = (lambda_q1[layer][None], lambda_k1[layer][None], lambda_q2[layer][None],
                    lambda_k2[layer][None], subln_g[layer][None])
        ln1 = (ln1_g[layer][None], ln1_b[layer][None])
        ln2 = (ln2_g[layer][None], ln2_b[layer][None])

        (fqT, fkaug, fvT, cqbT, dqT, dkaug, dvT, *p_new) = _project_prompt(
            hp, wm[layer], wf[layer], bfv[layer], tri, dkb, p_new, batch=batch, seq_len=seq_len, tm=tm_proj)
        fox = _fox_attention(fqT, cqbT, fkaug, fvT, batch=batch, seq_len=seq_len, blk=blk)
        diff = _diff_attention(dqT, dqbT, dkaug, dvT, *lam_args[:4], subln_g[layer][:, None], batch=batch,
                               seq_len=seq_len, blk=blk, lambda_init=lambda_init)
        h1, h1b, comb = _out_proj(fox, diff, hp, wo[layer], *ln1, wr, br, tm=tm_tok, dn_alpha=dn_alpha)
        hp = _moe_grouped(h1b, h1, comb, wg, wu, wd, *ln2, layer=layer, tm=tm_moe,
                          cap=min(MOE_CAP, tm_moe), dn_alpha=dn_alpha)

        ps, lfs = _project_sample(hs, wm[layer], wf[layer], bfv[layer])
        sfq, sfk, sfv, sdq, sdk, sdv = (ps[:, 512 * c:512 * (c + 1)] for c in range(6))
        lfs = lfs[:, :N_FOX_HEADS]
        sfox, sdiff = _decode_attention(page_table, layer, fkc, fvc, lfc, dkc, dvc, sfq, sfk, sfv, lfs * LOG2E,
                                        sdq, sdk, sdv, *lam_args, group=group, lambda_init=lambda_init)
        sfox = sfox.astype(BF16)
        sdiff = sdiff.astype(BF16)
        s1, s1b, scomb = _out_proj(sfox, sdiff, hs, wo[layer], *ln1, wr, br, tm=nseq, dn_alpha=dn_alpha)
        hs = _moe(s1b, s1, scomb, wg, wu, wd, *ln2, layer=layer, tm=nseq, dn_alpha=dn_alpha)
        for lst, val in zip(s_rows, (
                sfk.reshape(nseq, 1, N_FOX_HEADS, HEAD_DIM), sfv.reshape(nseq, 1, N_FOX_HEADS, HEAD_DIM),
                lfs.reshape(nseq, 1, N_FOX_HEADS),
                sdk.reshape(nseq, 1, N_DIFF_HEADS, 2 * HEAD_DIM), sdv.reshape(nseq, 1, N_DIFF_HEADS, 2 * HEAD_DIM))):
            lst.append(val)

    fk_t, fv_t, lf_t, dk_s, dv_s = p_new
    heads_t = lambda a: jnp.transpose(a.reshape(depth, batch, N_FOX_HEADS, HEAD_DIM, seq_len), (0, 1, 4, 2, 3))
    outs_p = [heads_t(fk_t), heads_t(fv_t), jnp.transpose(lf_t, (0, 1, 3, 2)),
              dk_s.reshape(depth, batch, seq_len, N_DIFF_HEADS, 2 * HEAD_DIM),
              dv_s.reshape(depth, batch, seq_len, N_DIFF_HEADS, 2 * HEAD_DIM)]
    outs_s = [jnp.stack(r, 0) for r in s_rows]
    return (hp.reshape(batch, seq_len, d_model), hs.reshape(nseq, 1, d_model), *outs_p, *outs_s)
```
